```python
import math
import jax, jax.numpy as jnp
from jax import lax
import numpy as np

D_MODEL = 2048
BATCH = 4
SEQ = 2048
DEPTH = 2
DEC_BATCH = 128
DEC_SEQ = 1
PAST_LEN = 16384
PAGE_SIZE = 128

N_AB = (DEPTH + 1) // 2
N_CD = DEPTH // 2

S5_WIDTH = D_MODEL // 2
S5_GROUP = 16
S5_GROUPS = S5_WIDTH // S5_GROUP
S5_STATE = 64
S5_DT_MIN = 1e-3
S5_DT_MAX = 1e-1
SG_HEADS = 8
SG_WIDTH = D_MODEL // 2
SG_HEAD_DIM = SG_WIDTH // SG_HEADS
SG_CHUNK = 128
AB_IN = S5_WIDTH + 2 * SG_WIDTH
AB_OUT = S5_WIDTH + SG_WIDTH
MLA_HEADS = 8
MLA_Q_RANK = D_MODEL // 4
MLA_KV_RANK = D_MODEL // 8
MLA_NOPE = 128
MLA_ROPE = 64
MLA_V = 128
DIFF_HEADS = 8
DIFF_KV_HEADS = 2
DIFF_GROUP = DIFF_HEADS // DIFF_KV_HEADS
DIFF_HEAD_DIM = 64
DIFF_V_DIM = 2 * DIFF_HEAD_DIM
CD_SPLITS = [MLA_Q_RANK,
             MLA_Q_RANK + MLA_KV_RANK,
             MLA_Q_RANK + MLA_KV_RANK + MLA_ROPE,
             MLA_Q_RANK + MLA_KV_RANK + MLA_ROPE + DIFF_HEADS * 2 * DIFF_HEAD_DIM,
             MLA_Q_RANK + MLA_KV_RANK + MLA_ROPE + DIFF_HEADS * 2 * DIFF_HEAD_DIM + DIFF_KV_HEADS * 2 * DIFF_HEAD_DIM]
CD_IN = CD_SPLITS[-1] + DIFF_KV_HEADS * DIFF_V_DIM
CD_OUT = MLA_HEADS * MLA_V + DIFF_HEADS * DIFF_V_DIM
N_EXPERTS = 32
TOP_K = 4
D_EXPERT = D_MODEL
SWIGLU_LIMIT = 7.0
SWIGLU_ALPHA = 1.702
MOE_BLOCK = 128
PLE_DIM = 256
ROPE_THETA = 10000.0
Q_BLOCK = 128
POOL_NUM = 5
POOL_DEN = 4
DEEPNORM_ALPHA = (2 * DEPTH) ** 0.25
DEEPNORM_BETA = (8 * DEPTH) ** -0.25
NEG_INF = -1e30

kernel_name = "hybrid_s5_gmlp_mla_diff_moe_step"


def _layernorm(x, g, b, eps=1e-5):
    xf = x.astype(jnp.float32)
    mu = jnp.mean(xf, -1, keepdims=True)
    var = jnp.mean(jnp.square(xf - mu), -1, keepdims=True)
    return ((xf - mu) * lax.rsqrt(var + eps)).astype(x.dtype) * g + b


def _rmsnorm(x, g, eps=1e-6):
    xf = x.astype(jnp.float32)
    return (xf * lax.rsqrt(jnp.mean(xf * xf, -1, keepdims=True) + eps)).astype(x.dtype) * g


def _rope(x, pos):
    d = x.shape[-1]
    inv = ROPE_THETA ** (-jnp.arange(0, d, 2, dtype=jnp.float32) / d)
    ang = pos.astype(jnp.float32)[:, None] * inv[None, :]
    ang = ang.reshape((ang.shape[0],) + (1,) * (x.ndim - 3) + (ang.shape[1],))
    cos, sin = jnp.cos(ang), jnp.sin(ang)
    x1, x2 = jnp.split(x.astype(jnp.float32), 2, axis=-1)
    return jnp.concatenate([x1 * cos - x2 * sin, x1 * sin + x2 * cos], -1).astype(x.dtype)


def _complex_affine_combine(e1, e2):
    a1r, a1i, b1r, b1i = e1
    a2r, a2i, b2r, b2i = e2
    return (a2r * a1r - a2i * a1i, a2r * a1i + a2i * a1r,
            a2r * b1r - a2i * b1i + b2r, a2r * b1i + a2i * b1r + b2i)


def _s5_scan(u, x0, lam_re, lam_im, b_re, b_im, c_re, c_im, d_skip, log_dt):
    f32 = jnp.float32
    bsz, t, _ = u.shape
    uf = u.astype(f32).reshape(bsz, t, S5_GROUPS, S5_GROUP)
    dt = jnp.exp(log_dt.astype(f32))[:, None]
    lr = jnp.minimum(lam_re.astype(f32), -1e-4)
    li = lam_im.astype(f32)
    mag = jnp.exp(lr * dt)
    ab_re, ab_im = mag * jnp.cos(li * dt), mag * jnp.sin(li * dt)
    nr = ab_re - 1.0
    den = lr * lr + li * li
    zr = (nr * lr + ab_im * li) / den
    zi = (ab_im * lr - nr * li) / den
    br, bi = b_re.astype(f32), b_im.astype(f32)
    bb_re = zr[..., None] * br - zi[..., None] * bi
    bb_im = zr[..., None] * bi + zi[..., None] * br
    h_re = jnp.einsum('btgc,gpc->btgp', uf, bb_re)
    h_im = jnp.einsum('btgc,gpc->btgp', uf, bb_im)
    a_re = jnp.broadcast_to(ab_re, h_re.shape)
    a_im = jnp.broadcast_to(ab_im, h_im.shape)
    A_re, A_im, X_re, X_im = lax.associative_scan(_complex_affine_combine, (a_re, a_im, h_re, h_im), axis=1)
    if x0 is not None:
        x0r = x0[0].astype(f32)[:, None]
        x0i = x0[1].astype(f32)[:, None]
        X_re, X_im = X_re + A_re * x0r - A_im * x0i, X_im + A_re * x0i + A_im * x0r
    y = (jnp.einsum('btgp,gcp->btgc', X_re, c_re.astype(f32))
         - jnp.einsum('btgp,gcp->btgc', X_im, c_im.astype(f32))
         + d_skip.astype(f32) * uf)
    return (y.reshape(bsz, t, S5_WIDTH).astype(u.dtype),
            X_re[:, -1].astype(u.dtype), X_im[:, -1].astype(u.dtype))


def _spatial_gating(z, ln_g, ln_b, w_s, b_s):
    bsz, t, _ = z.shape
    u, v = jnp.split(z, 2, axis=-1)
    v = _layernorm(v, ln_g, ln_b)
    t_pad = -(-t // SG_CHUNK) * SG_CHUNK
    vp = jnp.pad(v, ((0, 0), (0, t_pad - t), (0, 0)))
    vp = vp.reshape(bsz, t_pad // SG_CHUNK, SG_CHUNK, SG_HEADS, SG_HEAD_DIM)
    w = w_s * jnp.tril(jnp.ones((SG_CHUNK, SG_CHUNK), w_s.dtype))
    s = jnp.einsum('hts,bcshd->bcthd', w, vp) + b_s.T[:, :, None]
    s = s.reshape(bsz, t_pad, SG_WIDTH)[:, :t]
    return u * s, v


def _block_stats(s, mask, v, pv):
    s = jnp.where(mask, s.astype(jnp.float32), NEG_INF)
    m = jnp.max(s, axis=-1, keepdims=True)
    p = jnp.where(mask, jnp.exp(s - m), 0.0)
    return m, jnp.sum(p, -1, keepdims=True), pv(p, v)


def _causal_attention(q, k, v, scores, pv):
    bsz, t = q.shape[:2]
    qb = min(Q_BLOCK, t)
    nb = t // qb
    qs = jnp.moveaxis(q.reshape((bsz, nb, qb) + q.shape[2:]), 1, 0)
    kpos = jnp.arange(t)

    def block(args):
        qi, i = args
        qpos = i * qb + jnp.arange(qb)
        m, l, acc = _block_stats(scores(qi, k), qpos[:, None] >= kpos[None, :], v, pv)
        return acc / l

    out = lax.map(block, (qs, jnp.arange(nb)))
    out = jnp.moveaxis(out, 0, -3)
    return out.reshape(out.shape[:-3] + (t, out.shape[-1]))


def _paged_attention(q, k_new, v_new, fetch, page_table, scores, pv):
    tq = q.shape[1]

    def page(phys):
        kp, vp = fetch(phys)
        return _block_stats(scores(q, kp), jnp.ones((tq, kp.shape[1]), bool), vp, pv)

    mp, lp, ap = lax.map(page, page_table.T)
    ms, ls, a_s = _block_stats(scores(q, k_new), jnp.tril(jnp.ones((tq, tq), bool)), v_new, pv)
    m = jnp.maximum(jnp.max(mp, 0), ms)
    wp = jnp.exp(mp - m)
    ws = jnp.exp(ms - m)
    return (jnp.sum(wp * ap, 0) + ws * a_s) / (jnp.sum(wp * lp, 0) + ws * ls)


def _mixer_ab(x, x0, j, w):
    (ab_w_in, lam_re, lam_im, b_re, b_im, c_re, c_im, d_skip, log_dt, w_glu, b_glu,
     sg_ln_g, sg_ln_b, sg_w_s, sg_b_s, ab_w_out) = w
    t = x.shape[1]
    z = x @ ab_w_in[j]
    y5, s_re, s_im = _s5_scan(z[..., :S5_WIDTH], x0, lam_re[j], lam_im[j], b_re[j], b_im[j],
                              c_re[j], c_im[j], d_skip[j], log_dt[j])
    g5 = jax.nn.gelu(y5)
    o5 = g5 * jax.nn.sigmoid(g5 @ w_glu[j] + b_glu[j])
    osg, v = _spatial_gating(jax.nn.gelu(z[..., S5_WIDTH:]), sg_ln_g[j], sg_ln_b[j], sg_w_s[j], sg_b_s[j])
    out = jnp.concatenate([o5, osg], -1) @ ab_w_out[j]
    return out, s_re, s_im, v[:, ((t - 1) // SG_CHUNK) * SG_CHUNK:]


def _mixer_cd(x, pos, i, j, w, paged):
    (cd_w_in, mla_q_norm, mla_kv_norm, mla_w_uq, mla_w_uk, mla_w_uv,
     diff_lq1, diff_lk1, diff_lq2, diff_lk2, diff_subln, cd_w_out) = w
    f32 = jnp.float32
    bsz, t, _ = x.shape
    z = x @ cd_w_in[j]
    q_lat, kv_lat, k_pe, dq, dk, dv = jnp.split(z, CD_SPLITS, axis=-1)
    q = jnp.einsum('btr,rhd->bthd', _rmsnorm(q_lat, mla_q_norm[j]), mla_w_uq[j])
    q_nope, q_pe = q[..., :MLA_NOPE], _rope(q[..., MLA_NOPE:], pos)
    c_kv = _rmsnorm(kv_lat, mla_kv_norm[j])
    k_pe = _rope(k_pe, pos)
    q_cat = jnp.concatenate([jnp.einsum('bthd,rhd->bthr', q_nope, mla_w_uk[j]), q_pe], -1)
    k_cat = jnp.concatenate([c_kv, k_pe], -1)
    mla_scale = (MLA_NOPE + MLA_ROPE) ** -0.5
    mla_scores = lambda qq, kk: jnp.einsum('bqhd,bkd->bhqk', qq, kk) * mla_scale
    mla_pv = lambda p, vv: jnp.einsum('bhqk,bkc->bhqc', p, vv)
    dq = _rope(dq.reshape(bsz, t, DIFF_KV_HEADS, DIFF_GROUP, 2, DIFF_HEAD_DIM), pos)
    dk = _rope(dk.reshape(bsz, t, DIFF_KV_HEADS, 2, DIFF_HEAD_DIM), pos)
    dv = dv.reshape(bsz, t, DIFF_KV_HEADS, DIFF_V_DIM)
    diff_scale = DIFF_HEAD_DIM ** -0.5
    diff_scores = lambda qq, kk: jnp.einsum('bqhgmd,bkhmd->bhgmqk', qq, kk) * diff_scale
    diff_pv = lambda p, vv: jnp.einsum('bhgmqk,bkhv->bhgmqv', p, vv)
    if paged is None:
        o_mla = _causal_attention(q_cat, k_cat, c_kv, mla_scores, mla_pv)
        o_diff = _causal_attention(dq, dk, dv, diff_scores, diff_pv)
    else:
        lat_pool, pe_pool, dk_pool, dv_pool, page_table = paged

        def fetch_mla(ph):
            lat = lat_pool[j, ph]
            return jnp.concatenate([lat, pe_pool[j, ph]], -1), lat

        def fetch_diff(ph):
            return dk_pool[j, ph], dv_pool[j, ph]

        o_mla = _paged_attention(q_cat, k_cat, c_kv, fetch_mla, page_table, mla_scores, mla_pv)
        o_diff = _paged_attention(dq, dk, dv, fetch_diff, page_table, diff_scores, diff_pv)
    y_mla = jnp.einsum('bhqr,rhd->bqhd', o_mla, mla_w_uv[j]).reshape(bsz, t, MLA_HEADS * MLA_V)
    lam_init = 0.8 - 0.6 * math.exp(-0.3 * i)
    lam = (jnp.exp(jnp.sum(diff_lq1[j].astype(f32) * diff_lk1[j].astype(f32)))
           - jnp.exp(jnp.sum(diff_lq2[j].astype(f32) * diff_lk2[j].astype(f32))) + lam_init)
    o = o_diff[:, :, :, 0] - lam * o_diff[:, :, :, 1]
    o = _rmsnorm(o, diff_subln[j], eps=1e-5) * (1.0 - lam_init)
    y_diff = jnp.transpose(o, (0, 3, 1, 2, 4)).reshape(bsz, t, DIFF_HEADS * DIFF_V_DIM)
    out = jnp.concatenate([y_mla, y_diff], -1).astype(x.dtype) @ cd_w_out[j]
    return out, c_kv, k_pe, dk, dv


def _moe(x, i, w):
    router_w, router_b, wg, bg, wu, bu, wd, bd = w
    bsz, t, d = x.shape
    xt = x.reshape(-1, d)
    n = xt.shape[0]
    logits = (xt @ router_w[i] + router_b[i]).astype(jnp.float32)
    top_v, top_i = lax.top_k(logits, TOP_K)
    gates = jax.nn.softmax(top_v, axis=-1)
    nk = n * TOP_K
    flat_e = top_i.reshape(-1)
    flat_tok = jnp.arange(nk, dtype=jnp.int32) // TOP_K
    order = jnp.argsort(flat_e)
    se = flat_e[order]
    counts = jnp.bincount(flat_e, length=N_EXPERTS)
    padded = (counts + MOE_BLOCK - 1) // MOE_BLOCK * MOE_BLOCK
    pad_end = jnp.cumsum(padded)
    pad_start = pad_end - padded
    grp_start = jnp.cumsum(counts) - counts
    dest = pad_start[se] + jnp.arange(nk) - grp_start[se]
    n_blocks = -(-(nk + N_EXPERTS * (MOE_BLOCK - 1)) // MOE_BLOCK)
    rows = n_blocks * MOE_BLOCK
    tok = jnp.full((rows,), n, jnp.int32).at[dest].set(flat_tok[order])
    gw = jnp.zeros((rows,), jnp.float32).at[dest].set(gates.reshape(-1)[order])
    blk_e = jnp.minimum(jnp.searchsorted(pad_end, jnp.arange(n_blocks) * MOE_BLOCK, side='right'),
                        N_EXPERTS - 1)
    xpad = jnp.concatenate([xt, jnp.zeros((1, d), xt.dtype)], 0)
    xb = xpad[tok].reshape(n_blocks, MOE_BLOCK, d)

    def expert_block(args):
        xe, e = args
        g = jnp.minimum(xe @ wg[i, e] + bg[i, e], SWIGLU_LIMIT)
        u = jnp.clip(xe @ wu[i, e] + bu[i, e], -SWIGLU_LIMIT, SWIGLU_LIMIT)
        h = (u + 1.0) * (g * jax.nn.sigmoid(SWIGLU_ALPHA * g))
        return h @ wd[i, e] + bd[i, e]

    yb = lax.map(expert_block, (xb, blk_e)).reshape(rows, d)
    y = jnp.zeros((n + 1, d), jnp.float32).at[tok].add(yb.astype(jnp.float32) * gw[:, None])[:n]
    return y.astype(x.dtype).reshape(bsz, t, d)


def _post_block(x, mix, p, i, w_ln, w_moe, w_ple):
    ln1_g, ln1_b, ln2_g, ln2_b = w_ln
    ple_w_proj, ple_w_gate = w_ple
    x = _layernorm(DEEPNORM_ALPHA * x + mix, ln1_g[i], ln1_b[i])
    x = _layernorm(DEEPNORM_ALPHA * x + _moe(x, i, w_moe), ln2_g[i], ln2_b[i])
    return x + (p @ ple_w_proj[i]) * jax.nn.sigmoid(x @ ple_w_gate[i])


def setup_inputs(seed: int = 0) -> dict:
    key = jax.random.key(seed)
    keys = iter(jax.random.split(key, 64))
    f32 = jnp.float32

    def nrm(shape, scale=1.0):
        return jax.random.normal(next(keys), shape, f32) * scale

    def gain(shape):
        return 1.0 + nrm(shape, 0.02)

    n_pages = PAST_LEN // PAGE_SIZE
    n_pool = (DEC_BATCH * n_pages * POOL_NUM) // POOL_DEN
    x_prompt = nrm((BATCH, SEQ, D_MODEL))
    x_sample = nrm((DEC_BATCH, DEC_SEQ, D_MODEL))
    state_s5_re = nrm((N_AB, DEC_BATCH, S5_GROUPS, S5_STATE), 0.5)
    state_s5_im = nrm((N_AB, DEC_BATCH, S5_GROUPS, S5_STATE), 0.5)
    cache_mla_latent = nrm((N_CD, n_pool, PAGE_SIZE, MLA_KV_RANK))
    cache_mla_krope = nrm((N_CD, n_pool, PAGE_SIZE, MLA_ROPE))
    cache_diff_k = nrm((N_CD, n_pool, PAGE_SIZE, DIFF_KV_HEADS, 2, DIFF_HEAD_DIM))
    cache_diff_v = nrm((N_CD, n_pool, PAGE_SIZE, DIFF_KV_HEADS, DIFF_V_DIM))
    page_table = jax.random.permutation(next(keys), n_pool)[:DEC_BATCH * n_pages]
    page_table = page_table.reshape(DEC_BATCH, n_pages).astype(jnp.int32)
    p_prompt = nrm((DEPTH, BATCH, SEQ, PLE_DIM))
    p_sample = nrm((DEPTH, DEC_BATCH, DEC_SEQ, PLE_DIM))
    return {
        'x_prompt': x_prompt,
        'x_sample': x_sample,
        'state_s5_re': state_s5_re,
        'state_s5_im': state_s5_im,
        'cache_mla_latent': cache_mla_latent,
        'cache_mla_krope': cache_mla_krope,
        'cache_diff_k': cache_diff_k,
        'cache_diff_v': cache_diff_v,
        'page_table': page_table,
        'p_prompt': p_prompt,
        'p_sample': p_sample,
        'ab_w_in': nrm((N_AB, D_MODEL, AB_IN), D_MODEL ** -0.5),
        's5_lambda_re': -0.5 * jnp.exp(nrm((N_AB, S5_GROUPS, S5_STATE), 0.02)),
        's5_lambda_im': jnp.pi * jnp.arange(S5_STATE, dtype=f32) + nrm((N_AB, S5_GROUPS, S5_STATE), 0.01),
        's5_b_re': nrm((N_AB, S5_GROUPS, S5_STATE, S5_GROUP), (2 * S5_GROUP) ** -0.5),
        's5_b_im': nrm((N_AB, S5_GROUPS, S5_STATE, S5_GROUP), (2 * S5_GROUP) ** -0.5),
        's5_c_re': nrm((N_AB, S5_GROUPS, S5_GROUP, S5_STATE), (2 * S5_STATE) ** -0.5),
        's5_c_im': nrm((N_AB, S5_GROUPS, S5_GROUP, S5_STATE), (2 * S5_STATE) ** -0.5),
        's5_d': nrm((N_AB, S5_GROUPS, S5_GROUP)),
        's5_log_dt': jax.random.uniform(next(keys), (N_AB, S5_GROUPS), f32,
                                        math.log(S5_DT_MIN), math.log(S5_DT_MAX)),
        's5_w_glu': nrm((N_AB, S5_WIDTH, S5_WIDTH), S5_WIDTH ** -0.5),
        's5_b_glu': nrm((N_AB, S5_WIDTH), 0.01),
        'sg_ln_g': gain((N_AB, SG_WIDTH)),
        'sg_ln_b': nrm((N_AB, SG_WIDTH), 0.02),
        'sg_w_s': nrm((N_AB, SG_HEADS, SG_CHUNK, SG_CHUNK), 0.5 * SG_CHUNK ** -0.5),
        'sg_b_s': gain((N_AB, SG_HEADS, SG_CHUNK)),
        'ab_w_out': nrm((N_AB, AB_OUT, D_MODEL), AB_OUT ** -0.5 * DEEPNORM_BETA),
        'cd_w_in': nrm((N_CD, D_MODEL, CD_IN), D_MODEL ** -0.5),
        'mla_q_norm': gain((N_CD, MLA_Q_RANK)),
        'mla_kv_norm': gain((N_CD, MLA_KV_RANK)),
        'mla_w_uq': nrm((N_CD, MLA_Q_RANK, MLA_HEADS, MLA_NOPE + MLA_ROPE), MLA_Q_RANK ** -0.5),
        'mla_w_uk': nrm((N_CD, MLA_KV_RANK, MLA_HEADS, MLA_NOPE), MLA_KV_RANK ** -0.5),
        'mla_w_uv': nrm((N_CD, MLA_KV_RANK, MLA_HEADS, MLA_V), MLA_KV_RANK ** -0.5),
        'diff_lq1': nrm((N_CD, DIFF_HEAD_DIM), 0.1),
        'diff_lk1': nrm((N_CD, DIFF_HEAD_DIM), 0.1),
        'diff_lq2': nrm((N_CD, DIFF_HEAD_DIM), 0.1),
        'diff_lk2': nrm((N_CD, DIFF_HEAD_DIM), 0.1),
        'diff_subln': gain((N_CD, DIFF_V_DIM)),
        'cd_w_out': nrm((N_CD, CD_OUT, D_MODEL), CD_OUT ** -0.5 * DEEPNORM_BETA),
        'ln1_g': gain((DEPTH, D_MODEL)),
        'ln1_b': nrm((DEPTH, D_MODEL), 0.02),
        'ln2_g': gain((DEPTH, D_MODEL)),
        'ln2_b': nrm((DEPTH, D_MODEL), 0.02),
        'router_w': nrm((DEPTH, D_MODEL, N_EXPERTS), D_MODEL ** -0.5),
        'router_b': nrm((DEPTH, N_EXPERTS), 0.01),
        'ex_w_gate': nrm((DEPTH, N_EXPERTS, D_MODEL, D_EXPERT), D_MODEL ** -0.5),
        'ex_b_gate': nrm((DEPTH, N_EXPERTS, D_EXPERT), 0.01),
        'ex_w_up': nrm((DEPTH, N_EXPERTS, D_MODEL, D_EXPERT), D_MODEL ** -0.5),
        'ex_b_up': nrm((DEPTH, N_EXPERTS, D_EXPERT), 0.01),
        'ex_w_down': nrm((DEPTH, N_EXPERTS, D_EXPERT, D_MODEL), D_EXPERT ** -0.5 * DEEPNORM_BETA),
        'ex_b_down': nrm((DEPTH, N_EXPERTS, D_MODEL), 0.01),
        'ple_w_proj': nrm((DEPTH, PLE_DIM, D_MODEL), PLE_DIM ** -0.5),
        'ple_w_gate': nrm((DEPTH, D_MODEL, D_MODEL), D_MODEL ** -0.5),
    }


def reference(x_prompt, x_sample, state_s5_re, state_s5_im, cache_mla_latent, cache_mla_krope,
              cache_diff_k, cache_diff_v, page_table, p_prompt, p_sample,
              ab_w_in, s5_lambda_re, s5_lambda_im, s5_b_re, s5_b_im, s5_c_re, s5_c_im, s5_d, s5_log_dt,
              s5_w_glu, s5_b_glu, sg_ln_g, sg_ln_b, sg_w_s, sg_b_s, ab_w_out,
              cd_w_in, mla_q_norm, mla_kv_norm, mla_w_uq, mla_w_uk, mla_w_uv,
              diff_lq1, diff_lk1, diff_lq2, diff_lk2, diff_subln, cd_w_out,
              ln1_g, ln1_b, ln2_g, ln2_b, router_w, router_b,
              ex_w_gate, ex_b_gate, ex_w_up, ex_b_up, ex_w_down, ex_b_down,
              ple_w_proj, ple_w_gate):
    ab_w = (ab_w_in, s5_lambda_re, s5_lambda_im, s5_b_re, s5_b_im, s5_c_re, s5_c_im, s5_d, s5_log_dt,
            s5_w_glu, s5_b_glu, sg_ln_g, sg_ln_b, sg_w_s, sg_b_s, ab_w_out)
    cd_w = (cd_w_in, mla_q_norm, mla_kv_norm, mla_w_uq, mla_w_uk, mla_w_uv,
            diff_lq1, diff_lk1, diff_lq2, diff_lk2, diff_subln, cd_w_out)
    ln_w = (ln1_g, ln1_b, ln2_g, ln2_b)
    moe_w = (router_w, router_b, ex_w_gate, ex_b_gate, ex_w_up, ex_b_up, ex_w_down, ex_b_down)
    ple_w = (ple_w_proj, ple_w_gate)
    paged = (cache_mla_latent, cache_mla_krope, cache_diff_k, cache_diff_v, page_table)

    past_len = page_table.shape[1] * cache_mla_latent.shape[2]
    pos_p = jnp.arange(x_prompt.shape[1], dtype=jnp.int32)
    pos_s = past_len + jnp.arange(x_sample.shape[1], dtype=jnp.int32)

    xp, xs = x_prompt, x_sample
    s5p_re, s5p_im, s5s_re, s5s_im, sgp, sgs = [], [], [], [], [], []
    latp, pep, lats, pes, dkp, dvp, dks, dvs = [], [], [], [], [], [], [], []
    for i in range(DEPTH):
        j = i // 2
        if i % 2 == 0:
            mix_p, a_re, a_im, v_p = _mixer_ab(xp, None, j, ab_w)
            mix_s, b_re, b_im, v_s = _mixer_ab(xs, (state_s5_re[j], state_s5_im[j]), j, ab_w)
            s5p_re.append(a_re); s5p_im.append(a_im); s5s_re.append(b_re); s5s_im.append(b_im)
            sgp.append(v_p); sgs.append(v_s)
        else:
            mix_p, lat_p, pe_p, k_p, v_p = _mixer_cd(xp, pos_p, i, j, cd_w, None)
            mix_s, lat_s, pe_s, k_s, v_s = _mixer_cd(xs, pos_s, i, j, cd_w, paged)
            latp.append(lat_p); pep.append(pe_p); dkp.append(k_p); dvp.append(v_p)
            lats.append(lat_s); pes.append(pe_s); dks.append(k_s); dvs.append(v_s)
        xp = _post_block(xp, mix_p, p_prompt[i], i, ln_w, moe_w, ple_w)
        xs = _post_block(xs, mix_s, p_sample[i], i, ln_w, moe_w, ple_w)

    y_prompt, y_sample = xp, xs
    s5_re_prompt, s5_im_prompt = jnp.stack(s5p_re), jnp.stack(s5p_im)
    s5_re_sample, s5_im_sample = jnp.stack(s5s_re), jnp.stack(s5s_im)
    sg_v_prompt, sg_v_sample = jnp.stack(sgp), jnp.stack(sgs)
    mla_latent_prompt, mla_krope_prompt = jnp.stack(latp), jnp.stack(pep)
    mla_latent_sample, mla_krope_sample = jnp.stack(lats), jnp.stack(pes)
    diff_k_prompt, diff_v_prompt = jnp.stack(dkp), jnp.stack(dvp)
    diff_k_sample, diff_v_sample = jnp.stack(dks), jnp.stack(dvs)
    return (y_prompt, y_sample, s5_re_prompt, s5_im_prompt, s5_re_sample, s5_im_sample,
            sg_v_prompt, sg_v_sample, mla_latent_prompt, mla_krope_prompt, mla_latent_sample, mla_krope_sample,
            diff_k_prompt, diff_v_prompt, diff_k_sample, diff_v_sample)
```

```python
import functools
import math

import jax
import jax.numpy as jnp
from jax import lax
from jax.experimental import pallas as pl
from jax.experimental.pallas import tpu as pltpu

F32 = jnp.float32
BF16 = jnp.bfloat16

D_MODEL = 2048
DEPTH = 2
S5_WIDTH = 1024
S5_GROUP = 16
S5_GROUPS = 64
S5_STATE = 64
S5_CHUNK = 16
SG_HEADS = 8
SG_WIDTH = 1024
SG_CHUNK = 128
MLA_HEADS = 8
MLA_Q_RANK = 512
MLA_KV_RANK = 256
MLA_NOPE = 128
MLA_ROPE = 64
MLA_V = 128
DIFF_HEADS = 8
DIFF_KV_HEADS = 2
DIFF_GROUP = 4
DIFF_HEAD_DIM = 64
DIFF_V_DIM = 128
N_EXPERTS = 32
TOP_K = 4
SWIGLU_LIMIT = 7.0
SWIGLU_ALPHA = 1.702
PLE_DIM = 256
ROPE_THETA = 10000.0
Q_BLOCK = 128
ALPHA = (2 * DEPTH) ** 0.25
NEG_INF = -1e30

MOE_TM = 1024
MOE_SUB = 256
MOE_TF = 256
PAGES_PER_STEP = 8
VMEM_LIMIT = 56 * 1024 * 1024


def _cparams(sem):
    return pltpu.CompilerParams(dimension_semantics=sem, vmem_limit_bytes=VMEM_LIMIT)


def _row_tile(rows, target):
    best = 8
    for cand in range(8, min(rows, target) + 1, 8):
        if rows % cand == 0:
            best = cand
    return best


def _gelu(x):
    return 0.5 * x * (1.0 + jnp.tanh(0.7978845608028654 * (x + 0.044715 * (x * x * x))))


def _sigmoid(x):
    return 1.0 / (1.0 + jnp.exp(-x))


def _dot(a, b):
    return jnp.dot(a, b, preferred_element_type=F32)


def _dot_nt(a, b):
    return lax.dot_general(a, b, (((1,), (1,)), ((), ())), preferred_element_type=F32)


def _ln_rows(x, g, b, eps=1e-5):
    mu = jnp.mean(x, -1, keepdims=True)
    xc = x - mu
    var = jnp.mean(xc * xc, -1, keepdims=True)
    return xc * lax.rsqrt(var + eps) * g + b


def _mm_kernel(x_ref, w_ref, o_ref, *, gelu_from):
    acc = _dot(x_ref[...].astype(BF16), w_ref[...])
    if gelu_from is None:
        o_ref[...] = acc
    else:
        j = pl.program_id(0)

        @pl.when(j < gelu_from)
        def _():
            o_ref[...] = acc

        @pl.when(j >= gelu_from)
        def _():
            o_ref[...] = _gelu(acc)


def _mm(x, w, *, tm, tn, gelu_from=None):
    m, k = x.shape
    n = w.shape[1]
    tm = _row_tile(m, tm)
    return pl.pallas_call(
        functools.partial(_mm_kernel, gelu_from=gelu_from),
        grid=(n // tn, m // tm),
        in_specs=[pl.BlockSpec((tm, k), lambda j, i: (i, 0)),
                  pl.BlockSpec((k, tn), lambda j, i: (0, j))],
        out_specs=pl.BlockSpec((tm, tn), lambda j, i: (i, j)),
        out_shape=jax.ShapeDtypeStruct((m, n), F32),
        compiler_params=_cparams(("arbitrary", "arbitrary")),
        name="mm",
    )(x, w)


def _bmm_kernel(x_ref, w_ref, o_ref):
    o_ref[...] = _dot(x_ref[...].astype(BF16), w_ref[...])


def _bmm(x, w, *, tm):
    h, m, k = x.shape
    n = w.shape[2]
    tm = _row_tile(m, tm)
    return pl.pallas_call(
        _bmm_kernel,
        grid=(h, m // tm),
        in_specs=[pl.BlockSpec((None, tm, k), lambda a, i: (a, i, 0)),
                  pl.BlockSpec((None, k, n), lambda a, i: (a, 0, 0))],
        out_specs=pl.BlockSpec((None, tm, n), lambda a, i: (a, i, 0)),
        out_shape=jax.ShapeDtypeStruct((h, m, n), F32),
        compiler_params=_cparams(("arbitrary", "arbitrary")),
        name="bmm",
    )(x, w)


def _glu_kernel(g_ref, gcol_ref, w_ref, b_ref, o_ref):
    acc = _dot(g_ref[...].astype(BF16), w_ref[...]) + b_ref[...]
    o_ref[...] = gcol_ref[...] * _sigmoid(acc)


def _glu(g5, w, b, *, tm, tn):
    m, k = g5.shape
    n = w.shape[1]
    tm = _row_tile(m, tm)
    return pl.pallas_call(
        _glu_kernel,
        grid=(n // tn, m // tm),
        in_specs=[pl.BlockSpec((tm, k), lambda j, i: (i, 0)),
                  pl.BlockSpec((tm, tn), lambda j, i: (i, j)),
                  pl.BlockSpec((k, tn), lambda j, i: (0, j)),
                  pl.BlockSpec((1, tn), lambda j, i: (0, j))],
        out_specs=pl.BlockSpec((tm, tn), lambda j, i: (i, j)),
        out_shape=jax.ShapeDtypeStruct((m, n), F32),
        compiler_params=_cparams(("arbitrary", "arbitrary")),
        name="glu",
    )(g5, g5, w, b)


def _mix_ln_router_kernel(r_ref, a_ref, b_ref, wa_ref, wb_ref, g_ref, beta_ref,
                          wrh_ref, wrl_ref, br_ref, x_ref, lg_ref):
    mix = _dot(a_ref[...].astype(BF16), wa_ref[...]) + _dot(b_ref[...].astype(BF16), wb_ref[...])
    x = _ln_rows(ALPHA * r_ref[...] + mix, g_ref[...], beta_ref[...])
    x_ref[...] = x
    xh = x.astype(BF16)
    xl = (x - xh.astype(F32)).astype(BF16)
    lg_ref[...] = (_dot(xh, wrh_ref[...]) + _dot(xl, wrh_ref[...]) + _dot(xh, wrl_ref[...])
                   + br_ref[...])


def _mix_ln_router(r, a, b, wa, wb, g, beta, wr, br, *, tm):
    m, d = r.shape
    tm = _row_tile(m, tm)
    ka, kb = a.shape[1], b.shape[1]
    ne = wr.shape[1]
    wr_p = jnp.zeros((d, 128), F32).at[:, :ne].set(wr)
    wrh = wr_p.astype(BF16)
    wrl = (wr_p - wrh.astype(F32)).astype(BF16)
    br_p = jnp.zeros((1, 128), F32).at[0, :ne].set(br)
    row = lambda i: (i, 0)
    fix = lambda i: (0, 0)
    x, lg = pl.pallas_call(
        _mix_ln_router_kernel,
        grid=(m // tm,),
        in_specs=[pl.BlockSpec((tm, d), row), pl.BlockSpec((tm, ka), row), pl.BlockSpec((tm, kb), row),
                  pl.BlockSpec((ka, d), fix), pl.BlockSpec((kb, d), fix),
                  pl.BlockSpec((1, d), fix), pl.BlockSpec((1, d), fix),
                  pl.BlockSpec((d, 128), fix), pl.BlockSpec((d, 128), fix), pl.BlockSpec((1, 128), fix)],
        out_specs=[pl.BlockSpec((tm, d), row), pl.BlockSpec((tm, 128), row)],
        out_shape=[jax.ShapeDtypeStruct((m, d), F32), jax.ShapeDtypeStruct((m, 128), F32)],
        compiler_params=_cparams(("arbitrary",)),
        name="mix_ln_router",
    )(r, a, b, wa, wb, g.reshape(1, d), beta.reshape(1, d), wrh, wrl, br_p)
    return x, lg[:, :ne]


def _post_kernel(x1_ref, y_ref, p_ref, wp_ref, wg_ref, g_ref, beta_ref, o_ref):
    x = _ln_rows(ALPHA * x1_ref[...] + y_ref[...], g_ref[...], beta_ref[...])
    proj = _dot(p_ref[...].astype(BF16), wp_ref[...])
    gate = _sigmoid(_dot(x.astype(BF16), wg_ref[...]))
    o_ref[...] = x + proj * gate


def _post(x1, y, p, wp, wg, g, beta, *, tm):
    m, d = x1.shape
    tm = _row_tile(m, tm)
    kp = p.shape[1]
    row = lambda i: (i, 0)
    fix = lambda i: (0, 0)
    return pl.pallas_call(
        _post_kernel,
        grid=(m // tm,),
        in_specs=[pl.BlockSpec((tm, d), row), pl.BlockSpec((tm, d), row), pl.BlockSpec((tm, kp), row),
                  pl.BlockSpec((kp, d), fix), pl.BlockSpec((d, d), fix),
                  pl.BlockSpec((1, d), fix), pl.BlockSpec((1, d), fix)],
        out_specs=pl.BlockSpec((tm, d), row),
        out_shape=jax.ShapeDtypeStruct((m, d), F32),
        compiler_params=_cparams(("arbitrary",)),
        name="post",
    )(x1, y, p, wp, wg, g.reshape(1, d), beta.reshape(1, d))


def _s5_discretize(lam_re, lam_im, b_re, b_im, log_dt):
    dt = jnp.exp(log_dt)[:, None]
    lr = jnp.minimum(lam_re, -1e-4)
    li = lam_im
    mag = jnp.exp(lr * dt)
    ab_re, ab_im = mag * jnp.cos(li * dt), mag * jnp.sin(li * dt)
    nr = ab_re - 1.0
    den = lr * lr + li * li
    zr = (nr * lr + ab_im * li) / den
    zi = (ab_im * lr - nr * li) / den
    bb_re = zr[..., None] * b_re - zi[..., None] * b_im
    bb_im = zr[..., None] * b_im + zi[..., None] * b_re
    return lr * dt, li * dt, ab_re, ab_im, bb_re, bb_im


def _s5_chunk_operators(lam_re, lam_im, b_re, b_im, c_re, c_im, log_dt):
    L = S5_CHUNK
    G, P, H = S5_GROUPS, S5_STATE, S5_GROUP
    ldr, ldi, _, _, bb_re, bb_im = _s5_discretize(lam_re, lam_im, b_re, b_im, log_dt)
    k = jnp.arange(L + 1, dtype=F32)[:, None, None]
    pk_mag = jnp.exp(k * ldr)
    pk_re, pk_im = pk_mag * jnp.cos(k * ldi), pk_mag * jnp.sin(k * ldi)
    abb_re = pk_re[..., None] * bb_re - pk_im[..., None] * bb_im
    abb_im = pk_re[..., None] * bb_im + pk_im[..., None] * bb_re
    kern = (jnp.einsum('gcp,kgpd->kgcd', c_re, abb_re[:L]) - jnp.einsum('gcp,kgpd->kgcd', c_im, abb_im[:L]))
    t = jnp.arange(L)
    lag = t[None, :] - t[:, None]
    toe = jnp.where((lag >= 0)[:, :, None, None, None], kern[jnp.clip(lag, 0, L - 1)], 0.0)
    m_op = jnp.transpose(toe, (2, 0, 4, 1, 3)).reshape(G, L * H, L * H)
    rev_re = abb_re[L - 1 - t]
    rev_im = abb_im[L - 1 - t]
    bs_op = jnp.concatenate([jnp.transpose(rev_re, (1, 0, 3, 2)).reshape(G, L * H, P),
                             jnp.transpose(rev_im, (1, 0, 3, 2)).reshape(G, L * H, P)], -1)
    ca_re = c_re[None] * pk_re[1:, :, None, :] - c_im[None] * pk_im[1:, :, None, :]
    ca_im = c_re[None] * pk_im[1:, :, None, :] + c_im[None] * pk_re[1:, :, None, :]
    cs_op = jnp.concatenate([jnp.transpose(ca_re, (1, 3, 0, 2)).reshape(G, P, L * H),
                             -jnp.transpose(ca_im, (1, 3, 0, 2)).reshape(G, P, L * H)], 1)
    return m_op, bs_op, cs_op, pk_re[L], pk_im[L]


def _s5_state_kernel(u_ref, bs_ref, s_ref, *, gpb):
    for q in range(gpb):
        s_ref[:, q * 128:(q + 1) * 128] = _dot(u_ref[q].astype(BF16), bs_ref[q])


def _s5_scan_kernel(sr_ref, si_ref, ar_ref, ai_ref, x0r_ref, x0i_ref, er_ref, ei_ref):
    nchunk = sr_ref.shape[0]
    ar, ai = ar_ref[...], ai_ref[...]

    def body(j, carry):
        xr, xi = carry
        x0r_ref[j] = xr
        x0i_ref[j] = xi
        return ar * xr - ai * xi + sr_ref[j], ar * xi + ai * xr + si_ref[j]

    zero = jnp.zeros(sr_ref.shape[1:], F32)
    xr, xi = lax.fori_loop(0, nchunk, body, (zero, zero))
    er_ref[...] = xr
    ei_ref[...] = xi


def _s5_out_kernel(u_ref, x0_ref, m_ref, cs_ref, d_ref, o_ref, *, gpb):
    for q in range(gpb):
        u = u_ref[q]
        y = (_dot(u.astype(BF16), m_ref[q]) + _dot(x0_ref[:, q * 128:(q + 1) * 128].astype(BF16), cs_ref[q])
             + d_ref[q] * u)
        o_ref[q] = _gelu(y)


def _s5_prompt(u, ops, d_skip):
    m_op, bs_op, cs_op, al_re, al_im = ops
    bsz, t, _ = u.shape
    L, G, P, H = S5_CHUNK, S5_GROUPS, S5_STATE, S5_GROUP
    nchunk = t // L
    rows = nchunk * bsz
    gpb = 8
    ug = jnp.transpose(u.reshape(bsz, nchunk, L, G, H), (3, 1, 0, 2, 4)).reshape(G, rows, L * H)
    s_loc = pl.pallas_call(
        functools.partial(_s5_state_kernel, gpb=gpb),
        grid=(G // gpb,),
        in_specs=[pl.BlockSpec((gpb, rows, L * H), lambda g: (g, 0, 0)),
                  pl.BlockSpec((gpb, L * H, 2 * P), lambda g: (g, 0, 0))],
        out_specs=pl.BlockSpec((rows, gpb * 2 * P), lambda g: (0, g)),
        out_shape=jax.ShapeDtypeStruct((rows, G * 2 * P), F32),
        compiler_params=_cparams(("arbitrary",)),
        name="s5_state",
    )(ug, bs_op.astype(BF16))
    s4 = s_loc.reshape(nchunk, bsz, G, 2, P)
    sr = s4[:, :, :, 0].reshape(nchunk, bsz, G * P)
    si = s4[:, :, :, 1].reshape(nchunk, bsz, G * P)
    lw = 512
    seq = pl.BlockSpec((nchunk, bsz, lw), lambda i: (0, 0, i))
    vec = pl.BlockSpec((1, lw), lambda i: (0, i))
    fin = pl.BlockSpec((bsz, lw), lambda i: (0, i))
    x0r, x0i, er, ei = pl.pallas_call(
        _s5_scan_kernel,
        grid=(G * P // lw,),
        in_specs=[seq, seq, vec, vec],
        out_specs=[seq, seq, fin, fin],
        out_shape=[jax.ShapeDtypeStruct((nchunk, bsz, G * P), F32)] * 2
        + [jax.ShapeDtypeStruct((bsz, G * P), F32)] * 2,
        compiler_params=_cparams(("arbitrary",)),
        name="s5_scan",
    )(sr, si, al_re.reshape(1, G * P), al_im.reshape(1, G * P))
    x0 = jnp.stack([x0r.reshape(rows, G, P), x0i.reshape(rows, G, P)], 2).reshape(rows, G * 2 * P)
    d_t = jnp.tile(d_skip, (1, L)).reshape(G, 1, L * H)
    g5 = pl.pallas_call(
        functools.partial(_s5_out_kernel, gpb=gpb),
        grid=(G // gpb,),
        in_specs=[pl.BlockSpec((gpb, rows, L * H), lambda g: (g, 0, 0)),
                  pl.BlockSpec((rows, gpb * 2 * P), lambda g: (0, g)),
                  pl.BlockSpec((gpb, L * H, L * H), lambda g: (g, 0, 0)),
                  pl.BlockSpec((gpb, 2 * P, L * H), lambda g: (g, 0, 0)),
                  pl.BlockSpec((gpb, 1, L * H), lambda g: (g, 0, 0))],
        out_specs=pl.BlockSpec((gpb, rows, L * H), lambda g: (g, 0, 0)),
        out_shape=jax.ShapeDtypeStruct((G, rows, L * H), F32),
        compiler_params=_cparams(("arbitrary",)),
        name="s5_out",
    )(ug, x0, m_op.astype(BF16), cs_op.astype(BF16), d_t)
    g5 = jnp.transpose(g5.reshape(G, nchunk, bsz, L, H), (2, 1, 3, 0, 4)).reshape(bsz, t, G * H)
    return g5, er.reshape(bsz, G, P), ei.reshape(bsz, G, P)


def _s5_step_kernel(u_ref, x0r_ref, x0i_ref, ar_ref, ai_ref, bbr_ref, bbi_ref, cr_ref, ci_ref, d_ref,
                    g_ref, xr_ref, xi_ref, *, nblk, bw, sw):
    for q in range(nblk):
        u = u_ref[:, q * bw:(q + 1) * bw]
        ub = u.astype(BF16)
        sl = slice(q * sw, (q + 1) * sw)
        ar, ai = ar_ref[:, sl], ai_ref[:, sl]
        x0r, x0i = x0r_ref[:, sl], x0i_ref[:, sl]
        xr = ar * x0r - ai * x0i + _dot(ub, bbr_ref[q])
        xi = ar * x0i + ai * x0r + _dot(ub, bbi_ref[q])
        xr_ref[:, sl] = xr
        xi_ref[:, sl] = xi
        y = _dot(xr.astype(BF16), cr_ref[q]) - _dot(xi.astype(BF16), ci_ref[q]) + d_ref[:, q * bw:(q + 1) * bw] * u
        g_ref[:, q * bw:(q + 1) * bw] = _gelu(y)


def _s5_step(u, x0r, x0i, lam_re, lam_im, b_re, b_im, c_re, c_im, log_dt, d_skip):
    n = u.shape[0]
    G, P, H = S5_GROUPS, S5_STATE, S5_GROUP
    gb = 16
    nblk = G // gb
    _, _, ab_re, ab_im, bb_re, bb_im = _s5_discretize(lam_re, lam_im, b_re, b_im, log_dt)
    eye = jnp.eye(gb, dtype=F32)

    def in_blocks(w):
        w4 = w.reshape(nblk, gb, P, H)
        return jnp.einsum('qgph,gk->qghkp', w4, eye).reshape(nblk, gb * H, gb * P)

    def out_blocks(w):
        w4 = w.reshape(nblk, gb, H, P)
        return jnp.einsum('qghp,gk->qgpkh', w4, eye).reshape(nblk, gb * P, gb * H)

    full = lambda i: (0, 0)
    full3 = lambda i: (0, 0, 0)
    g5, xr, xi = pl.pallas_call(
        functools.partial(_s5_step_kernel, nblk=nblk, bw=gb * H, sw=gb * P),
        grid=(1,),
        in_specs=[pl.BlockSpec((n, G * H), full), pl.BlockSpec((n, G * P), full), pl.BlockSpec((n, G * P), full),
                  pl.BlockSpec((1, G * P), full), pl.BlockSpec((1, G * P), full),
                  pl.BlockSpec((nblk, gb * H, gb * P), full3), pl.BlockSpec((nblk, gb * H, gb * P), full3),
                  pl.BlockSpec((nblk, gb * P, gb * H), full3), pl.BlockSpec((nblk, gb * P, gb * H), full3),
                  pl.BlockSpec((1, G * H), full)],
        out_specs=[pl.BlockSpec((n, G * H), full), pl.BlockSpec((n, G * P), full), pl.BlockSpec((n, G * P), full)],
        out_shape=[jax.ShapeDtypeStruct((n, G * H), F32), jax.ShapeDtypeStruct((n, G * P), F32),
                   jax.ShapeDtypeStruct((n, G * P), F32)],
        compiler_params=_cparams(("arbitrary",)),
        name="s5_step",
    )(u, x0r.reshape(n, G * P), x0i.reshape(n, G * P), ab_re.reshape(1, G * P), ab_im.reshape(1, G * P),
      in_blocks(bb_re).astype(BF16), in_blocks(bb_im).astype(BF16),
      out_blocks(c_re).astype(BF16), out_blocks(c_im).astype(BF16), d_skip.reshape(1, G * H))
    return g5, xr.reshape(n, G, P), xi.reshape(n, G, P)


def _sg_kernel(u_ref, v_ref, g_ref, b_ref, w_ref, bias_ref, o_ref, vl_ref, *, cpb):
    i = pl.program_id(0)
    v = _ln_rows(v_ref[...], g_ref[...], b_ref[...])
    vb = v.astype(BF16)
    hd = SG_WIDTH // SG_HEADS
    for h in range(SG_HEADS):
        sl = slice(h * hd, (h + 1) * hd)
        s = _dot(w_ref[h], vb[:, sl]) + bias_ref[:, sl]
        o_ref[:, sl] = u_ref[:, sl] * s

    @pl.when(i % cpb == cpb - 1)
    def _():
        vl_ref[...] = v


def _spatial_gating_prompt(z, col0, bsz, t, ln_g, ln_b, w_s, b_s):
    cpb = t // SG_CHUNK
    cu = col0 // SG_WIDTH
    w = (w_s * jnp.tril(jnp.ones((SG_CHUNK, SG_CHUNK), F32))).astype(BF16)
    bias = jnp.repeat(b_s.T, SG_WIDTH // SG_HEADS, axis=1)
    fix = lambda i: (0, 0)
    return pl.pallas_call(
        functools.partial(_sg_kernel, cpb=cpb),
        grid=(bsz * cpb,),
        in_specs=[pl.BlockSpec((SG_CHUNK, SG_WIDTH), lambda i: (i, cu)),
                  pl.BlockSpec((SG_CHUNK, SG_WIDTH), lambda i: (i, cu + 1)),
                  pl.BlockSpec((1, SG_WIDTH), fix), pl.BlockSpec((1, SG_WIDTH), fix),
                  pl.BlockSpec((SG_HEADS, SG_CHUNK, SG_CHUNK), lambda i: (0, 0, 0)),
                  pl.BlockSpec((SG_CHUNK, SG_WIDTH), fix)],
        out_specs=[pl.BlockSpec((SG_CHUNK, SG_WIDTH), lambda i: (i, 0)),
                   pl.BlockSpec((None, SG_CHUNK, SG_WIDTH), lambda i: (i // cpb, 0, 0))],
        out_shape=[jax.ShapeDtypeStruct((bsz * t, SG_WIDTH), F32),
                   jax.ShapeDtypeStruct((bsz, SG_CHUNK, SG_WIDTH), F32)],
        compiler_params=_cparams(("arbitrary",)),
        name="spatial_gating",
    )(z, z, ln_g.reshape(1, -1), ln_b.reshape(1, -1), w, bias)


def _sg_step_kernel(u_ref, v_ref, g_ref, b_ref, w0_ref, b0_ref, o_ref, vn_ref):
    v = _ln_rows(v_ref[...], g_ref[...], b_ref[...])
    vn_ref[...] = v
    o_ref[...] = u_ref[...] * (w0_ref[...] * v + b0_ref[...])


def _spatial_gating_step(zs, ln_g, ln_b, w_s, b_s):
    n = zs.shape[0]
    hd = SG_WIDTH // SG_HEADS
    w0 = jnp.repeat(w_s[:, 0, 0], hd).reshape(1, SG_WIDTH)
    b0 = jnp.repeat(b_s[:, 0], hd).reshape(1, SG_WIDTH)
    fix = lambda i: (0, 0)
    return pl.pallas_call(
        _sg_step_kernel,
        grid=(1,),
        in_specs=[pl.BlockSpec((n, SG_WIDTH), lambda i: (0, 0)), pl.BlockSpec((n, SG_WIDTH), lambda i: (0, 1)),
                  pl.BlockSpec((1, SG_WIDTH), fix), pl.BlockSpec((1, SG_WIDTH), fix),
                  pl.BlockSpec((1, SG_WIDTH), fix), pl.BlockSpec((1, SG_WIDTH), fix)],
        out_specs=[pl.BlockSpec((n, SG_WIDTH), fix), pl.BlockSpec((n, SG_WIDTH), fix)],
        out_shape=[jax.ShapeDtypeStruct((n, SG_WIDTH), F32)] * 2,
        compiler_params=_cparams(("arbitrary",)),
        name="spatial_gating_step",
    )(zs, zs, ln_g.reshape(1, -1), ln_b.reshape(1, -1), w0, b0)


def _online_softmax_update(s, m_sc, l_sc):
    m_old = m_sc[...]
    m_new = jnp.maximum(m_old, jnp.max(s, -1, keepdims=True))
    a = jnp.exp(m_old - m_new)
    p = jnp.exp(s - m_new)
    l_sc[...] = a * l_sc[...] + jnp.sum(p, -1, keepdims=True)
    m_sc[...] = m_new
    return a, p


def _causal_mask(s, qi, kb, tk):
    row = lax.broadcasted_iota(jnp.int32, s.shape, 0)
    col = lax.broadcasted_iota(jnp.int32, s.shape, 1)
    qpos = qi * Q_BLOCK + (row & (Q_BLOCK - 1))
    return jnp.where(qpos >= kb * tk + col, s, NEG_INF)


def _mla_prefill_kernel(ql_ref, qp_ref, ckv_ref, kpe_ref, wuv_ref, y_ref, m_sc, l_sc, acc_sc, *, tk, scale):
    qi, kb = pl.program_id(1), pl.program_id(2)

    @pl.when(kb == 0)
    def _():
        m_sc[...] = jnp.full(m_sc.shape, NEG_INF, F32)
        l_sc[...] = jnp.zeros(l_sc.shape, F32)
        acc_sc[...] = jnp.zeros(acc_sc.shape, F32)

    @pl.when(kb * tk <= qi * Q_BLOCK + Q_BLOCK - 1)
    def _():
        k1 = ckv_ref[...].astype(BF16)
        s = (_dot_nt(ql_ref[...], k1) + _dot_nt(qp_ref[...], kpe_ref[...].astype(BF16))) * scale
        s = _causal_mask(s, qi, kb, tk)
        a, p = _online_softmax_update(s, m_sc, l_sc)
        acc_sc[...] = a * acc_sc[...] + _dot(p.astype(BF16), k1)

    @pl.when(kb == pl.num_programs(2) - 1)
    def _():
        o = (acc_sc[...] / l_sc[...]).astype(BF16)
        for h in range(MLA_HEADS):
            y_ref[:, h * MLA_V:(h + 1) * MLA_V] = _dot(o[h * Q_BLOCK:(h + 1) * Q_BLOCK],
                                                        wuv_ref[:, h * MLA_V:(h + 1) * MLA_V])


def _mla_prefill(q_lat, q_pe, c_kv, k_pe, w_uv, *, tk=512):
    bsz, t, nh, r = q_lat.shape
    nq = t // Q_BLOCK
    tk = min(tk, t)
    rows = nh * Q_BLOCK

    def blocks(q):
        d = q.shape[-1]
        return jnp.transpose(q.reshape(bsz, nq, Q_BLOCK, nh, d), (0, 1, 3, 2, 4)).reshape(bsz, nq, rows, d).astype(BF16)

    kmap = lambda b, qi, kb: (b, jnp.minimum(kb, (qi * Q_BLOCK + Q_BLOCK - 1) // tk), 0)
    scale = (MLA_NOPE + MLA_ROPE) ** -0.5
    return pl.pallas_call(
        functools.partial(_mla_prefill_kernel, tk=tk, scale=scale),
        grid=(bsz, nq, t // tk),
        in_specs=[pl.BlockSpec((None, None, rows, r), lambda b, qi, kb: (b, qi, 0, 0)),
                  pl.BlockSpec((None, None, rows, MLA_ROPE), lambda b, qi, kb: (b, qi, 0, 0)),
                  pl.BlockSpec((None, tk, r), kmap),
                  pl.BlockSpec((None, tk, MLA_ROPE), kmap),
                  pl.BlockSpec((r, nh * MLA_V), lambda b, qi, kb: (0, 0))],
        out_specs=pl.BlockSpec((None, Q_BLOCK, nh * MLA_V), lambda b, qi, kb: (b, qi, 0)),
        out_shape=jax.ShapeDtypeStruct((bsz, t, nh * MLA_V), F32),
        scratch_shapes=[pltpu.VMEM((rows, 1), F32), pltpu.VMEM((rows, 1), F32), pltpu.VMEM((rows, r), F32)],
        compiler_params=_cparams(("arbitrary", "arbitrary", "arbitrary")),
        name="mla_prefill",
    )(blocks(q_lat), blocks(q_pe), c_kv, k_pe, w_uv.reshape(r, nh * MLA_V).astype(BF16))


def _diff_finalize(o0, o1, lam, subln, lam_init):
    o = o0 - lam * o1
    ms = jnp.mean(o * o, -1, keepdims=True)
    return o * lax.rsqrt(ms + 1e-5) * subln * (1.0 - lam_init)


def _diff_prefill_kernel(q_ref, k_ref, v_ref, lam_ref, sub_ref, y_ref, m_sc, l_sc, acc_sc, *, tk, scale, lam_init):
    qi, kb = pl.program_id(1), pl.program_id(2)

    @pl.when(kb == 0)
    def _():
        m_sc[...] = jnp.full(m_sc.shape, NEG_INF, F32)
        l_sc[...] = jnp.zeros(l_sc.shape, F32)
        acc_sc[...] = jnp.zeros(acc_sc.shape, F32)

    @pl.when(kb * tk <= qi * Q_BLOCK + Q_BLOCK - 1)
    def _():
        vb = v_ref[...].astype(BF16)
        for mi in range(2):
            s = _dot_nt(q_ref[mi], k_ref[mi].astype(BF16)) * scale
            s = _causal_mask(s, qi, kb, tk)
            a, p = _online_softmax_update(s, m_sc.at[mi], l_sc.at[mi])
            acc_sc[mi] = a * acc_sc[mi] + _dot(p.astype(BF16), vb)

    @pl.when(kb == pl.num_programs(2) - 1)
    def _():
        o = _diff_finalize(acc_sc[0] / l_sc[0], acc_sc[1] / l_sc[1], lam_ref[...], sub_ref[...], lam_init)
        for g in range(DIFF_GROUP):
            y_ref[:, g * DIFF_V_DIM:(g + 1) * DIFF_V_DIM] = o[g * Q_BLOCK:(g + 1) * Q_BLOCK]


def _diff_prefill(dq, dk, dv, lam, subln, lam_init, *, tk=512):
    bsz, t = dq.shape[:2]
    nq = t // Q_BLOCK
    tk = min(tk, t)
    rows = DIFF_GROUP * Q_BLOCK
    d = DIFF_HEAD_DIM
    q = jnp.transpose(dq.reshape(bsz, nq, Q_BLOCK, DIFF_KV_HEADS, DIFF_GROUP, 2, d), (0, 3, 1, 5, 4, 2, 6))
    q = q.reshape(bsz * DIFF_KV_HEADS, nq, 2, rows, d).astype(BF16)
    k = jnp.transpose(dk, (0, 2, 3, 1, 4)).reshape(bsz * DIFF_KV_HEADS, 2, t, d)
    v = jnp.transpose(dv, (0, 2, 1, 3)).reshape(bsz * DIFF_KV_HEADS, t, DIFF_V_DIM)
    last = lambda qi: (qi * Q_BLOCK + Q_BLOCK - 1) // tk
    width = DIFF_GROUP * DIFF_V_DIM
    return pl.pallas_call(
        functools.partial(_diff_prefill_kernel, tk=tk, scale=d ** -0.5, lam_init=lam_init),
        grid=(bsz * DIFF_KV_HEADS, nq, t // tk),
        in_specs=[pl.BlockSpec((None, None, 2, rows, d), lambda a, qi, kb: (a, qi, 0, 0, 0)),
                  pl.BlockSpec((None, 2, tk, d), lambda a, qi, kb: (a, 0, jnp.minimum(kb, last(qi)), 0)),
                  pl.BlockSpec((None, tk, DIFF_V_DIM), lambda a, qi, kb: (a, jnp.minimum(kb, last(qi)), 0)),
                  pl.BlockSpec((1, DIFF_V_DIM), lambda a, qi, kb: (0, 0)),
                  pl.BlockSpec((1, DIFF_V_DIM), lambda a, qi, kb: (0, 0))],
        out_specs=pl.BlockSpec((None, Q_BLOCK, width),
                               lambda a, qi, kb: (a // DIFF_KV_HEADS, qi, a % DIFF_KV_HEADS)),
        out_shape=jax.ShapeDtypeStruct((bsz, t, DIFF_KV_HEADS * width), F32),
        scratch_shapes=[pltpu.VMEM((2, rows, 1), F32), pltpu.VMEM((2, rows, 1), F32),
                        pltpu.VMEM((2, rows, DIFF_V_DIM), F32)],
        compiler_params=_cparams(("arbitrary", "arbitrary", "arbitrary")),
        name="diff_prefill",
    )(q, k, v, jnp.full((1, DIFF_V_DIM), lam, F32), subln.reshape(1, DIFF_V_DIM))


def _paged_kernel(pt_ref, *refs, npg, has_q2, shared_v, scale):
    del pt_ref
    refs = list(refs)
    q1_ref = refs.pop(0)
    q2_ref = refs.pop(0) if has_q2 else None
    s0_ref, v0_ref = refs.pop(0), refs.pop(0)
    k1_refs = [refs.pop(0) for _ in range(npg)]
    k2_refs = [refs.pop(0) for _ in range(npg)] if has_q2 else None
    v_refs = k1_refs if shared_v else [refs.pop(0) for _ in range(npg)]
    o_ref, m_sc, l_sc, acc_sc = refs
    step = pl.program_id(1)

    @pl.when(step == 0)
    def _():
        m_sc[...] = s0_ref[...]
        l_sc[...] = jnp.ones(l_sc.shape, F32)
        acc_sc[...] = jnp.broadcast_to(v0_ref[...], acc_sc.shape)

    q1 = q1_ref[...]
    pages = []
    scores = []
    for i in range(npg):
        kb = k1_refs[i][...].astype(BF16)
        s = _dot_nt(q1, kb)
        if has_q2:
            s = s + _dot_nt(q2_ref[...], k2_refs[i][...].astype(BF16))
        pages.append(kb if shared_v else v_refs[i][...].astype(BF16))
        scores.append(s)
    s = jnp.concatenate(scores, axis=1) * scale
    a, p = _online_softmax_update(s, m_sc, l_sc)
    pb = p.astype(BF16)
    pw = scores[0].shape[1]
    acc = a * acc_sc[...]
    for i in range(npg):
        acc = acc + _dot(pb[:, i * pw:(i + 1) * pw], pages[i])
    acc_sc[...] = acc

    @pl.when(step == pl.num_programs(1) - 1)
    def _():
        o_ref[...] = acc_sc[...] / l_sc[...]


def _paged_attention(page_table, q1, q2, s0, v0, k1_pool, k2_pool, v_pool, *, scale):
    bsz, npages = page_table.shape
    npg = math.gcd(PAGES_PER_STEP, npages)
    has_q2 = q2 is not None
    shared_v = v_pool is None
    r, d1 = q1.shape[1:]
    page = k1_pool.shape[1]
    dv = v0.shape[2]

    def pool_specs(pool):
        return [pl.BlockSpec((None, page, pool.shape[2]),
                             functools.partial(lambda b, s, pt, i: (pt[b, s * npg + i], 0, 0), i=i))
                for i in range(npg)]

    per_b = lambda b, s, pt: (b, 0, 0)
    in_specs = [pl.BlockSpec((None, r, d1), per_b)]
    args = [q1.astype(BF16)]
    if has_q2:
        in_specs.append(pl.BlockSpec((None, r, q2.shape[2]), per_b))
        args.append(q2.astype(BF16))
    in_specs += [pl.BlockSpec((None, r, 1), per_b), pl.BlockSpec((None, 1, dv), per_b)]
    args += [s0, v0]
    in_specs += pool_specs(k1_pool)
    args += [k1_pool] * npg
    if has_q2:
        in_specs += pool_specs(k2_pool)
        args += [k2_pool] * npg
    if not shared_v:
        in_specs += pool_specs(v_pool)
        args += [v_pool] * npg
    grid_spec = pltpu.PrefetchScalarGridSpec(
        num_scalar_prefetch=1,
        grid=(bsz, npages // npg),
        in_specs=in_specs,
        out_specs=pl.BlockSpec((None, r, dv), per_b),
        scratch_shapes=[pltpu.VMEM((r, 1), F32), pltpu.VMEM((r, 1), F32), pltpu.VMEM((r, dv), F32)],
    )
    return pl.pallas_call(
        functools.partial(_paged_kernel, npg=npg, has_q2=has_q2, shared_v=shared_v, scale=scale),
        grid_spec=grid_spec,
        out_shape=jax.ShapeDtypeStruct((bsz, r, dv), F32),
        compiler_params=_cparams(("arbitrary", "arbitrary")),
        name="paged_attention",
    )(page_table, *args)


def _moe_kernel(blk_ref, e_ref, nsub_ref, x_ref, wg_ref, bg_ref, wu_ref, bu_ref, wd_ref, bd_ref, o_ref,
                wg_sc, wu_sc, wd_sc):
    del blk_ref, e_ref
    s, j = pl.program_id(0), pl.program_id(1)
    ns = nsub_ref[s]

    @pl.when(ns > 0)
    def _():
        @pl.when(j == 0)
        def _():
            o_ref[...] = jnp.broadcast_to(bd_ref[...], o_ref.shape)

        wg_sc[...] = wg_ref[...].astype(BF16)
        wu_sc[...] = wu_ref[...].astype(BF16)
        wd_sc[...] = wd_ref[...].astype(BF16)

        def body(i, c):
            r = pl.multiple_of(i * MOE_SUB, MOE_SUB)
            xb = x_ref[pl.ds(r, MOE_SUB), :]
            g = jnp.minimum(_dot(xb, wg_sc[...]) + bg_ref[...], SWIGLU_LIMIT)
            u = jnp.clip(_dot(xb, wu_sc[...]) + bu_ref[...], -SWIGLU_LIMIT, SWIGLU_LIMIT)
            h = (u + 1.0) * (g * _sigmoid(SWIGLU_ALPHA * g))
            o_ref[pl.ds(r, MOE_SUB), :] += _dot(h.astype(BF16), wd_sc[...])
            return c

        lax.fori_loop(0, ns, body, 0)


def _moe(x, logits, layer, wg, bg, wu, bu, wd, bd):
    n, d = x.shape
    f = wg.shape[-1]
    tm, sub, tf = MOE_TM, MOE_SUB, MOE_TF
    nf = f // tf
    nk = n * TOP_K
    nsb = N_EXPERTS + nk // tm
    top_v, top_i = lax.top_k(logits, TOP_K)
    gates = jax.nn.softmax(top_v, axis=-1)
    flat_e = top_i.reshape(-1).astype(jnp.int32)
    order = jnp.argsort(flat_e).astype(jnp.int32)
    se = flat_e[order]
    counts = jnp.bincount(flat_e, length=N_EXPERTS).astype(jnp.int32)
    sb_per_e = (counts + tm - 1) // tm
    sb_end = jnp.cumsum(sb_per_e).astype(jnp.int32)
    sb_start = sb_end - sb_per_e
    grp_start = (jnp.cumsum(counts) - counts).astype(jnp.int32)
    dest_sorted = sb_start[se] * tm + jnp.arange(nk, dtype=jnp.int32) - grp_start[se]
    rows = nsb * tm
    tok = jnp.full((rows,), n, jnp.int32).at[dest_sorted].set(order // TOP_K)
    xpad = jnp.concatenate([x.astype(BF16), jnp.zeros((1, d), BF16)], 0)
    xs = xpad[tok]
    n_used = sb_end[-1]
    sb_ids = jnp.arange(nsb, dtype=jnp.int32)
    used = sb_ids < n_used
    blk = jnp.minimum(sb_ids, n_used - 1)
    sb_e = jnp.minimum(jnp.searchsorted(sb_end, blk, side='right'), N_EXPERTS - 1).astype(jnp.int32)
    valid = jnp.clip(counts[sb_e] - (blk - sb_start[sb_e]) * tm, 0, tm)
    nsub = jnp.where(used, (valid + sub - 1) // sub, 0).astype(jnp.int32)

    jj = lambda s, j, ns: jnp.where(ns[s] > 0, j, nf - 1)
    grid_spec = pltpu.PrefetchScalarGridSpec(
        num_scalar_prefetch=3,
        grid=(nsb, nf),
        in_specs=[pl.BlockSpec((tm, d), lambda s, j, b, e, ns: (b[s], 0)),
                  pl.BlockSpec((None, None, d, tf), lambda s, j, b, e, ns: (layer, e[s], 0, jj(s, j, ns))),
                  pl.BlockSpec((None, None, 1, tf), lambda s, j, b, e, ns: (layer, e[s], 0, jj(s, j, ns))),
                  pl.BlockSpec((None, None, d, tf), lambda s, j, b, e, ns: (layer, e[s], 0, jj(s, j, ns))),
                  pl.BlockSpec((None, None, 1, tf), lambda s, j, b, e, ns: (layer, e[s], 0, jj(s, j, ns))),
                  pl.BlockSpec((None, None, tf, d), lambda s, j, b, e, ns: (layer, e[s], jj(s, j, ns), 0)),
                  pl.BlockSpec((None, None, 1, d), lambda s, j, b, e, ns: (layer, e[s], 0, 0))],
        out_specs=pl.BlockSpec((tm, d), lambda s, j, b, e, ns: (b[s], 0)),
        scratch_shapes=[pltpu.VMEM((d, tf), BF16), pltpu.VMEM((d, tf), BF16), pltpu.VMEM((tf, d), BF16)],
    )
    nl, ne = wg.shape[:2]
    yb = pl.pallas_call(
        _moe_kernel,
        grid_spec=grid_spec,
        out_shape=jax.ShapeDtypeStruct((rows, d), F32),
        compiler_params=_cparams(("arbitrary", "arbitrary")),
        name="moe_experts",
    )(blk, sb_e, nsub, xs, wg, bg.reshape(nl, ne, 1, f), wu, bu.reshape(nl, ne, 1, f), wd, bd.reshape(nl, ne, 1, d))
    dest = jnp.zeros((nk,), jnp.int32).at[order].set(dest_sorted)
    y = jnp.sum(yb[dest].reshape(n, TOP_K, d) * gates[..., None], axis=1)
    return y


def _rmsnorm(x, g, eps=1e-6):
    return x * lax.rsqrt(jnp.mean(x * x, -1, keepdims=True) + eps) * g


def _rope(x, pos):
    d = x.shape[-1]
    inv = ROPE_THETA ** (-jnp.arange(0, d, 2, dtype=F32) / d)
    ang = pos.astype(F32)[:, None] * inv[None, :]
    ang = ang.reshape((ang.shape[0],) + (1,) * (x.ndim - 2) + (ang.shape[1],))
    cos, sin = jnp.cos(ang), jnp.sin(ang)
    x1, x2 = jnp.split(x, 2, axis=-1)
    return jnp.concatenate([x1 * cos - x2 * sin, x1 * sin + x2 * cos], -1)


def _layer_ab(x, np_rows, bsz, t, state_re, state_im, w):
    (w_in, lam_re, lam_im, b_re, b_im, c_re, c_im, d_skip, log_dt, w_glu, b_glu,
     ln_g, ln_b, w_s, b_s) = w
    z = _mm(x, w_in.astype(BF16), tm=640, tn=512, gelu_from=S5_WIDTH // 512)
    ops = _s5_chunk_operators(lam_re, lam_im, b_re, b_im, c_re, c_im, log_dt)
    g5_p, pr, pi = _s5_prompt(z[:np_rows, :S5_WIDTH].reshape(bsz, t, S5_WIDTH), ops, d_skip)
    g5_s, sr, si = _s5_step(z[np_rows:, :S5_WIDTH], state_re, state_im,
                            lam_re, lam_im, b_re, b_im, c_re, c_im, log_dt, d_skip)
    g5 = jnp.concatenate([g5_p.reshape(np_rows, S5_WIDTH), g5_s], 0)
    o5 = _glu(g5, w_glu.astype(BF16), b_glu.reshape(1, -1), tm=640, tn=512)
    osg_p, v_p = _spatial_gating_prompt(z, S5_WIDTH, bsz, t, ln_g, ln_b, w_s, b_s)
    osg_s, v_s = _spatial_gating_step(z[np_rows:, S5_WIDTH:], ln_g, ln_b, w_s, b_s)
    osg = jnp.concatenate([osg_p, osg_s], 0)
    return o5, osg, (pr, pi, sr, si, v_p, v_s[:, None, :])


def _layer_cd(x, np_rows, bsz, t, pos, layer, caches, page_table, w):
    (w_in, q_norm, kv_norm, w_uq, w_uk, w_uv, lq1, lk1, lq2, lk2, subln) = w
    lat_pool, pe_pool, dk_pool, dv_pool = caches
    n = x.shape[0]
    ns = n - np_rows
    nh = MLA_HEADS
    a0, a1, a2 = MLA_Q_RANK, MLA_Q_RANK + MLA_KV_RANK, MLA_Q_RANK + MLA_KV_RANK + MLA_ROPE
    ndq = DIFF_HEADS * 2 * DIFF_HEAD_DIM
    ndk = DIFF_KV_HEADS * 2 * DIFF_HEAD_DIM
    ndv = DIFF_KV_HEADS * DIFF_V_DIM
    w_perm = jnp.concatenate([w_in[:, :a1], w_in[:, a2:], w_in[:, a1:a2],
                              jnp.zeros((w_in.shape[0], 128 - MLA_ROPE), F32)], 1).astype(BF16)
    z = _mm(x, w_perm, tm=640, tn=w_perm.shape[1])
    c = 0
    q_lat = z[:, c:c + MLA_Q_RANK]; c += MLA_Q_RANK
    kv_lat = z[:, c:c + MLA_KV_RANK]; c += MLA_KV_RANK
    dq = z[:, c:c + ndq]; c += ndq
    dk = z[:, c:c + ndk]; c += ndk
    dv = z[:, c:c + ndv]; c += ndv
    k_pe = z[:, c:c + MLA_ROPE]
    q = _mm(_rmsnorm(q_lat, q_norm), w_uq.reshape(MLA_Q_RANK, -1).astype(BF16), tm=640, tn=512)
    q = q.reshape(n, nh, MLA_NOPE + MLA_ROPE)
    q_pe = _rope(q[..., MLA_NOPE:], pos)
    q_abs = _bmm(jnp.transpose(q[..., :MLA_NOPE], (1, 0, 2)),
                 jnp.transpose(w_uk, (1, 2, 0)).astype(BF16), tm=640)
    q_abs = jnp.transpose(q_abs, (1, 0, 2))
    c_kv = _rmsnorm(kv_lat, kv_norm)
    k_pe = _rope(k_pe, pos)
    dq = _rope(dq.reshape(n, DIFF_KV_HEADS, DIFF_GROUP, 2, DIFF_HEAD_DIM), pos)
    dk = _rope(dk.reshape(n, DIFF_KV_HEADS, 2, DIFF_HEAD_DIM), pos)
    dv = dv.reshape(n, DIFF_KV_HEADS, DIFF_V_DIM)
    lam_init = 0.8 - 0.6 * math.exp(-0.3 * layer)
    lam = jnp.exp(jnp.sum(lq1 * lk1)) - jnp.exp(jnp.sum(lq2 * lk2)) + lam_init

    pr = lambda a: a[:np_rows].reshape((bsz, t) + a.shape[1:])
    y_mla_p = _mla_prefill(pr(q_abs), pr(q_pe), pr(c_kv), pr(k_pe), w_uv)
    y_diff_p = _diff_prefill(pr(dq), pr(dk), pr(dv), lam, subln, lam_init)

    sm = lambda a: a[np_rows:]
    mla_scale = (MLA_NOPE + MLA_ROPE) ** -0.5
    s0 = (jnp.einsum('bhr,br->bh', sm(q_abs), sm(c_kv)) + jnp.einsum('bhd,bd->bh', sm(q_pe), sm(k_pe))) * mla_scale
    rpad = 16 - nh
    padr = lambda a: jnp.pad(a, ((0, 0), (0, rpad)) + ((0, 0),) * (a.ndim - 2))
    o_mla = _paged_attention(page_table, padr(sm(q_abs)), padr(sm(q_pe)), padr(s0)[..., None],
                             sm(c_kv)[:, None, :], lat_pool, pe_pool, None, scale=mla_scale)[:, :nh]
    y_mla_s = _bmm(jnp.transpose(o_mla, (1, 0, 2)), jnp.transpose(w_uv, (1, 0, 2)).astype(BF16), tm=ns)
    y_mla_s = jnp.transpose(y_mla_s, (1, 0, 2)).reshape(ns, nh * MLA_V)

    diff_scale = DIFF_HEAD_DIM ** -0.5
    dq_s, dk_s, dv_s = sm(dq), sm(dk), sm(dv)
    qrow = jnp.transpose(dq_s, (0, 1, 3, 2, 4))
    sel = jnp.eye(DIFF_KV_HEADS * 2, dtype=F32).reshape(DIFF_KV_HEADS, 2, 1, DIFF_KV_HEADS * 2, 1)
    qblk = (qrow[:, :, :, :, None, :] * sel[None]).reshape(ns, 16, DIFF_KV_HEADS * 2 * DIFF_HEAD_DIM)
    s0d = jnp.einsum('brk,bk->br', qblk, dk_s.reshape(ns, -1)) * diff_scale
    o_d = _paged_attention(page_table, qblk, None, s0d[..., None], dv_s.reshape(ns, 1, -1),
                           dk_pool, None, dv_pool, scale=diff_scale)
    o_d = o_d.reshape(ns, DIFF_KV_HEADS, 2, DIFF_GROUP, DIFF_KV_HEADS, DIFF_V_DIM)
    o_d = jnp.stack([o_d[:, h, :, :, h] for h in range(DIFF_KV_HEADS)], 1)
    y_diff_s = _diff_finalize(o_d[:, :, 0], o_d[:, :, 1], lam, subln, lam_init).reshape(ns, -1)

    y_mla = jnp.concatenate([y_mla_p.reshape(np_rows, -1), y_mla_s], 0)
    y_diff = jnp.concatenate([y_diff_p.reshape(np_rows, -1), y_diff_s], 0)
    outs = (pr(c_kv), pr(k_pe), sm(c_kv)[:, None], sm(k_pe)[:, None],
            pr(dk), pr(dv), sm(dk)[:, None], sm(dv)[:, None])
    return y_mla, y_diff, outs


def kernel(x_prompt, x_sample, state_s5_re, state_s5_im, cache_mla_latent, cache_mla_krope, cache_diff_k, cache_diff_v, page_table, p_prompt, p_sample, ab_w_in, s5_lambda_re, s5_lambda_im, s5_b_re, s5_b_im, s5_c_re, s5_c_im, s5_d, s5_log_dt, s5_w_glu, s5_b_glu, sg_ln_g, sg_ln_b, sg_w_s, sg_b_s, ab_w_out, cd_w_in, mla_q_norm, mla_kv_norm, mla_w_uq, mla_w_uk, mla_w_uv, diff_lq1, diff_lk1, diff_lq2, diff_lk2, diff_subln, cd_w_out, ln1_g, ln1_b, ln2_g, ln2_b, router_w, router_b, ex_w_gate, ex_b_gate, ex_w_up, ex_b_up, ex_w_down, ex_b_down, ple_w_proj, ple_w_gate):
    bsz, t, d = x_prompt.shape
    nb, ts, _ = x_sample.shape
    assert ts == 1
    np_rows = bsz * t
    npool, page = cache_mla_latent.shape[1:3]
    past_len = page_table.shape[1] * page
    pos = jnp.concatenate([jnp.tile(jnp.arange(t, dtype=jnp.int32), bsz),
                           jnp.full((nb,), past_len, jnp.int32)])
    x = jnp.concatenate([x_prompt.reshape(np_rows, d), x_sample.reshape(nb, d)], 0)
    p_all = jnp.concatenate([p_prompt.reshape(DEPTH, np_rows, -1), p_sample.reshape(DEPTH, nb, -1)], 1)
    ab_outs, cd_outs = [], []
    for i in range(DEPTH):
        j = i // 2
        if i % 2 == 0:
            w = (ab_w_in[j], s5_lambda_re[j], s5_lambda_im[j], s5_b_re[j], s5_b_im[j], s5_c_re[j], s5_c_im[j],
                 s5_d[j], s5_log_dt[j], s5_w_glu[j], s5_b_glu[j], sg_ln_g[j], sg_ln_b[j], sg_w_s[j], sg_b_s[j])
            a, b, outs = _layer_ab(x, np_rows, bsz, t, state_s5_re[j], state_s5_im[j], w)
            ab_outs.append(outs)
            w_out = ab_w_out[j]
        else:
            w = (cd_w_in[j], mla_q_norm[j], mla_kv_norm[j], mla_w_uq[j], mla_w_uk[j], mla_w_uv[j],
                 diff_lq1[j], diff_lk1[j], diff_lq2[j], diff_lk2[j], diff_subln[j])
            caches = (cache_mla_latent[j].reshape(npool, page, -1), cache_mla_krope[j].reshape(npool, page, -1),
                      cache_diff_k[j].reshape(npool, page, -1), cache_diff_v[j].reshape(npool, page, -1))
            a, b, outs = _layer_cd(x, np_rows, bsz, t, pos, i, caches, page_table, w)
            cd_outs.append(outs)
            w_out = cd_w_out[j]
        ka = a.shape[1]
        x1, logits = _mix_ln_router(x, a, b, w_out[:ka].astype(BF16), w_out[ka:].astype(BF16),
                                    ln1_g[i], ln1_b[i], router_w[i], router_b[i], tm=520)
        y = _moe(x1, logits, i, ex_w_gate, ex_b_gate, ex_w_up, ex_b_up, ex_w_down, ex_b_down)
        x = _post(x1, y, p_all[i], ple_w_proj[i].astype(BF16), ple_w_gate[i].astype(BF16),
                  ln2_g[i], ln2_b[i], tm=520)
    y_prompt = x[:np_rows].reshape(bsz, t, d)
    y_sample = x[np_rows:].reshape(nb, ts, d)
    st = lambda k: jnp.stack([o[k] for o in ab_outs])
    sc = lambda k: jnp.stack([o[k] for o in cd_outs])
    return (y_prompt, y_sample, st(0), st(1), st(2), st(3), st(4), st(5),
            sc(0), sc(1), sc(2), sc(3), sc(4), sc(5), sc(6), sc(7))
```

```python
import functools
import math

import jax
import jax.numpy as jnp
from jax import lax
from jax.experimental import pallas as pl
from jax.experimental.pallas import tpu as pltpu

F32 = jnp.float32
BF16 = jnp.bfloat16
I32 = jnp.int32

D_MODEL = 2048
DEPTH = 2
S5_WIDTH = 1024
S5_GROUP = 16
S5_GROUPS = 64
S5_STATE = 64
S5_CHUNK = 16
SG_HEADS = 8
SG_WIDTH = 1024
SG_CHUNK = 128
MLA_HEADS = 8
MLA_Q_RANK = 512
MLA_KV_RANK = 256
MLA_NOPE = 128
MLA_ROPE = 64
MLA_V = 128
DIFF_HEADS = 8
DIFF_KV_HEADS = 2
DIFF_GROUP = 4
DIFF_HEAD_DIM = 64
DIFF_V_DIM = 128
N_EXPERTS = 32
TOP_K = 4
SWIGLU_LIMIT = 7.0
SWIGLU_ALPHA = 1.702
PLE_DIM = 256
ROPE_THETA = 10000.0
Q_BLOCK = 128
ALPHA = (2 * DEPTH) ** 0.25
NEG_INF = -1e30

MOE_SUB = 256
MOE_NSUB = 4
MOE_TF = 256
PAGES_PER_STEP = 32
DECODE_ROWS = 16
VMEM_LIMIT = 56 * 1024 * 1024


def _cparams(sem):
    return pltpu.CompilerParams(dimension_semantics=sem, vmem_limit_bytes=VMEM_LIMIT)


def _row_tile(rows, target):
    best = 8
    for cand in range(8, min(rows, target) + 1, 8):
        if rows % cand == 0:
            best = cand
    return best


def _gelu(x):
    return 0.5 * x * (1.0 + jnp.tanh(0.7978845608028654 * (x + 0.044715 * (x * x * x))))


def _sigmoid(x):
    return 1.0 / (1.0 + jnp.exp(-x))


def _dot(a, b):
    return jnp.dot(a, b, preferred_element_type=F32)


def _dot_nt(a, b):
    return lax.dot_general(a, b, (((1,), (1,)), ((), ())), preferred_element_type=F32)


def _ln_rows(x, g, b, eps=1e-5):
    mu = jnp.mean(x, -1, keepdims=True)
    xc = x - mu
    var = jnp.mean(xc * xc, -1, keepdims=True)
    return xc * lax.rsqrt(var + eps) * g + b


def _mm_kernel(x_ref, w_ref, o_ref, *, gelu_from):
    acc = _dot(x_ref[...].astype(BF16), w_ref[...])
    if gelu_from is None:
        o_ref[...] = acc
    else:
        j = pl.program_id(0)

        @pl.when(j < gelu_from)
        def _():
            o_ref[...] = acc

        @pl.when(j >= gelu_from)
        def _():
            o_ref[...] = _gelu(acc)


def _mm(x, w, *, tm, tn, gelu_from=None):
    m, k = x.shape
    n = w.shape[1]
    tm = _row_tile(m, tm)
    return pl.pallas_call(
        functools.partial(_mm_kernel, gelu_from=gelu_from),
        grid=(n // tn, m // tm),
        in_specs=[pl.BlockSpec((tm, k), lambda j, i: (i, 0)),
                  pl.BlockSpec((k, tn), lambda j, i: (0, j))],
        out_specs=pl.BlockSpec((tm, tn), lambda j, i: (i, j)),
        out_shape=jax.ShapeDtypeStruct((m, n), F32),
        compiler_params=_cparams(("arbitrary", "arbitrary")),
        name="mm",
    )(x, w)


def _bmm_kernel(x_ref, w_ref, o_ref):
    o_ref[...] = _dot(x_ref[...].astype(BF16), w_ref[...])


def _bmm(x, w, *, tm):
    h, m, k = x.shape
    n = w.shape[2]
    tm = _row_tile(m, tm)
    return pl.pallas_call(
        _bmm_kernel,
        grid=(h, m // tm),
        in_specs=[pl.BlockSpec((None, tm, k), lambda a, i: (a, i, 0)),
                  pl.BlockSpec((None, k, n), lambda a, i: (a, 0, 0))],
        out_specs=pl.BlockSpec((None, tm, n), lambda a, i: (a, i, 0)),
        out_shape=jax.ShapeDtypeStruct((h, m, n), F32),
        compiler_params=_cparams(("arbitrary", "arbitrary")),
        name="bmm",
    )(x, w)


def _glu_kernel(g_ref, gcol_ref, w_ref, b_ref, o_ref):
    acc = _dot(g_ref[...].astype(BF16), w_ref[...]) + b_ref[...]
    o_ref[...] = gcol_ref[...] * _sigmoid(acc)


def _glu(g5, w, b, *, tm, tn):
    m, k = g5.shape
    n = w.shape[1]
    tm = _row_tile(m, tm)
    return pl.pallas_call(
        _glu_kernel,
        grid=(n // tn, m // tm),
        in_specs=[pl.BlockSpec((tm, k), lambda j, i: (i, 0)),
                  pl.BlockSpec((tm, tn), lambda j, i: (i, j)),
                  pl.BlockSpec((k, tn), lambda j, i: (0, j)),
                  pl.BlockSpec((1, tn), lambda j, i: (0, j))],
        out_specs=pl.BlockSpec((tm, tn), lambda j, i: (i, j)),
        out_shape=jax.ShapeDtypeStruct((m, n), F32),
        compiler_params=_cparams(("arbitrary", "arbitrary")),
        name="glu",
    )(g5, g5, w, b)


def _mix_ln_router_kernel(r_ref, a_ref, b_ref, wa_ref, wb_ref, g_ref, beta_ref,
                          wrh_ref, wrl_ref, br_ref, x_ref, lg_ref):
    mix = _dot(a_ref[...].astype(BF16), wa_ref[...]) + _dot(b_ref[...].astype(BF16), wb_ref[...])
    x = _ln_rows(ALPHA * r_ref[...] + mix, g_ref[...], beta_ref[...])
    x_ref[...] = x
    xh = x.astype(BF16)
    xl = (x - xh.astype(F32)).astype(BF16)
    lg_ref[...] = (_dot(xh, wrh_ref[...]) + _dot(xl, wrh_ref[...]) + _dot(xh, wrl_ref[...])
                   + br_ref[...])


def _mix_ln_router(r, a, b, wa, wb, g, beta, wr, br, *, tm):
    m, d = r.shape
    tm = _row_tile(m, tm)
    ka, kb = a.shape[1], b.shape[1]
    ne = wr.shape[1]
    wr_p = jnp.zeros((d, 128), F32).at[:, :ne].set(wr)
    wrh = wr_p.astype(BF16)
    wrl = (wr_p - wrh.astype(F32)).astype(BF16)
    br_p = jnp.zeros((1, 128), F32).at[0, :ne].set(br)
    row = lambda i: (i, 0)
    fix = lambda i: (0, 0)
    x, lg = pl.pallas_call(
        _mix_ln_router_kernel,
        grid=(m // tm,),
        in_specs=[pl.BlockSpec((tm, d), row), pl.BlockSpec((tm, ka), row), pl.BlockSpec((tm, kb), row),
                  pl.BlockSpec((ka, d), fix), pl.BlockSpec((kb, d), fix),
                  pl.BlockSpec((1, d), fix), pl.BlockSpec((1, d), fix),
                  pl.BlockSpec((d, 128), fix), pl.BlockSpec((d, 128), fix), pl.BlockSpec((1, 128), fix)],
        out_specs=[pl.BlockSpec((tm, d), row), pl.BlockSpec((tm, 128), row)],
        out_shape=[jax.ShapeDtypeStruct((m, d), F32), jax.ShapeDtypeStruct((m, 128), F32)],
        compiler_params=_cparams(("arbitrary",)),
        name="mix_ln_router",
    )(r, a, b, wa, wb, g.reshape(1, d), beta.reshape(1, d), wrh, wrl, br_p)
    return x, lg[:, :ne]


def _post_kernel(x1_ref, *refs):
    ys_refs = refs[:TOP_K]
    gt_ref, p_ref, wp_ref, wg_ref, g_ref, beta_ref, o_ref = refs[TOP_K:]
    y = gt_ref[:, 0:1] * ys_refs[0][...]
    for k in range(1, TOP_K):
        y = y + gt_ref[:, k:k + 1] * ys_refs[k][...]
    x = _ln_rows(ALPHA * x1_ref[...] + y, g_ref[...], beta_ref[...])
    proj = _dot(p_ref[...].astype(BF16), wp_ref[...])
    gate = _sigmoid(_dot(x.astype(BF16), wg_ref[...]))
    o_ref[...] = x + proj * gate


def _post(x1, ysel, gates, p, wp, wg, g, beta, *, tm):
    m, d = x1.shape
    tm = _row_tile(m, tm)
    nb = m // tm
    kp = p.shape[1]
    gt = jnp.zeros((m, 128), F32).at[:, :TOP_K].set(gates)
    row = lambda i: (i, 0)
    fix = lambda i: (0, 0)
    ys_specs = [pl.BlockSpec((tm, d), functools.partial(lambda i, k: (k * nb + i, 0), k=k)) for k in range(TOP_K)]
    return pl.pallas_call(
        _post_kernel,
        grid=(nb,),
        in_specs=[pl.BlockSpec((tm, d), row)] + ys_specs + [
            pl.BlockSpec((tm, 128), row), pl.BlockSpec((tm, kp), row), pl.BlockSpec((kp, d), fix),
            pl.BlockSpec((d, d), fix), pl.BlockSpec((1, d), fix), pl.BlockSpec((1, d), fix)],
        out_specs=pl.BlockSpec((tm, d), row),
        out_shape=jax.ShapeDtypeStruct((m, d), F32),
        compiler_params=_cparams(("arbitrary",)),
        name="post",
    )(x1, *([ysel] * TOP_K), gt, p, wp, wg, g.reshape(1, d), beta.reshape(1, d))


def _s5_discretize(lam_re, lam_im, b_re, b_im, log_dt):
    dt = jnp.exp(log_dt)[:, None]
    lr = jnp.minimum(lam_re, -1e-4)
    li = lam_im
    mag = jnp.exp(lr * dt)
    ab_re, ab_im = mag * jnp.cos(li * dt), mag * jnp.sin(li * dt)
    nr = ab_re - 1.0
    den = lr * lr + li * li
    zr = (nr * lr + ab_im * li) / den
    zi = (ab_im * lr - nr * li) / den
    bb_re = zr[..., None] * b_re - zi[..., None] * b_im
    bb_im = zr[..., None] * b_im + zi[..., None] * b_re
    return lr * dt, li * dt, ab_re, ab_im, bb_re, bb_im


def _s5_chunk_operators(lam_re, lam_im, b_re, b_im, c_re, c_im, log_dt):
    L = S5_CHUNK
    G, P, H = S5_GROUPS, S5_STATE, S5_GROUP
    ldr, ldi, _, _, bb_re, bb_im = _s5_discretize(lam_re, lam_im, b_re, b_im, log_dt)
    k = jnp.arange(L + 1, dtype=F32)[:, None, None]
    pk_mag = jnp.exp(k * ldr)
    pk_re, pk_im = pk_mag * jnp.cos(k * ldi), pk_mag * jnp.sin(k * ldi)
    abb_re = pk_re[..., None] * bb_re - pk_im[..., None] * bb_im
    abb_im = pk_re[..., None] * bb_im + pk_im[..., None] * bb_re
    kern = (jnp.einsum('gcp,kgpd->kgcd', c_re, abb_re[:L]) - jnp.einsum('gcp,kgpd->kgcd', c_im, abb_im[:L]))
    t = jnp.arange(L)
    lag = t[None, :] - t[:, None]
    toe = jnp.where((lag >= 0)[:, :, None, None, None], kern[jnp.clip(lag, 0, L - 1)], 0.0)
    m_op = jnp.transpose(toe, (2, 0, 4, 1, 3)).reshape(G, L * H, L * H)
    rev_re = abb_re[L - 1 - t]
    rev_im = abb_im[L - 1 - t]
    bs_op = jnp.concatenate([jnp.transpose(rev_re, (1, 0, 3, 2)).reshape(G, L * H, P),
                             jnp.transpose(rev_im, (1, 0, 3, 2)).reshape(G, L * H, P)], -1)
    ca_re = c_re[None] * pk_re[1:, :, None, :] - c_im[None] * pk_im[1:, :, None, :]
    ca_im = c_re[None] * pk_im[1:, :, None, :] + c_im[None] * pk_re[1:, :, None, :]
    cs_op = jnp.concatenate([jnp.transpose(ca_re, (1, 3, 0, 2)).reshape(G, P, L * H),
                             -jnp.transpose(ca_im, (1, 3, 0, 2)).reshape(G, P, L * H)], 1)
    return m_op, bs_op, cs_op, pk_re[L], pk_im[L]


def _s5_state_kernel(u_ref, bs_ref, s_ref, *, gpb):
    for q in range(gpb):
        s_ref[:, q * 128:(q + 1) * 128] = _dot(u_ref[q].astype(BF16), bs_ref[q])


def _s5_scan_kernel(sr_ref, si_ref, ar_ref, ai_ref, x0r_ref, x0i_ref, er_ref, ei_ref):
    nchunk = sr_ref.shape[0]
    ar, ai = ar_ref[...], ai_ref[...]

    def body(j, carry):
        xr, xi = carry
        x0r_ref[j] = xr
        x0i_ref[j] = xi
        return ar * xr - ai * xi + sr_ref[j], ar * xi + ai * xr + si_ref[j]

    zero = jnp.zeros(sr_ref.shape[1:], F32)
    xr, xi = lax.fori_loop(0, nchunk, body, (zero, zero))
    er_ref[...] = xr
    ei_ref[...] = xi


def _s5_out_kernel(u_ref, x0_ref, m_ref, cs_ref, d_ref, o_ref, *, gpb):
    for q in range(gpb):
        u = u_ref[q]
        y = (_dot(u.astype(BF16), m_ref[q]) + _dot(x0_ref[:, q * 128:(q + 1) * 128].astype(BF16), cs_ref[q])
             + d_ref[q] * u)
        o_ref[q] = _gelu(y)


def _s5_prompt(u, ops, d_skip):
    m_op, bs_op, cs_op, al_re, al_im = ops
    bsz, t, _ = u.shape
    L, G, P, H = S5_CHUNK, S5_GROUPS, S5_STATE, S5_GROUP
    nchunk = t // L
    rows = nchunk * bsz
    gpb = 8
    ug = jnp.transpose(u.reshape(bsz, nchunk, L, G, H), (3, 1, 0, 2, 4)).reshape(G, rows, L * H)
    s_loc = pl.pallas_call(
        functools.partial(_s5_state_kernel, gpb=gpb),
        grid=(G // gpb,),
        in_specs=[pl.BlockSpec((gpb, rows, L * H), lambda g: (g, 0, 0)),
                  pl.BlockSpec((gpb, L * H, 2 * P), lambda g: (g, 0, 0))],
        out_specs=pl.BlockSpec((rows, gpb * 2 * P), lambda g: (0, g)),
        out_shape=jax.ShapeDtypeStruct((rows, G * 2 * P), F32),
        compiler_params=_cparams(("arbitrary",)),
        name="s5_state",
    )(ug, bs_op.astype(BF16))
    s4 = s_loc.reshape(nchunk, bsz, G, 2, P)
    sr = s4[:, :, :, 0].reshape(nchunk, bsz, G * P)
    si = s4[:, :, :, 1].reshape(nchunk, bsz, G * P)
    lw = 512
    seq = pl.BlockSpec((nchunk, bsz, lw), lambda i: (0, 0, i))
    vec = pl.BlockSpec((1, lw), lambda i: (0, i))
    fin = pl.BlockSpec((bsz, lw), lambda i: (0, i))
    x0r, x0i, er, ei = pl.pallas_call(
        _s5_scan_kernel,
        grid=(G * P // lw,),
        in_specs=[seq, seq, vec, vec],
        out_specs=[seq, seq, fin, fin],
        out_shape=[jax.ShapeDtypeStruct((nchunk, bsz, G * P), F32)] * 2
        + [jax.ShapeDtypeStruct((bsz, G * P), F32)] * 2,
        compiler_params=_cparams(("arbitrary",)),
        name="s5_scan",
    )(sr, si, al_re.reshape(1, G * P), al_im.reshape(1, G * P))
    x0 = jnp.stack([x0r.reshape(rows, G, P), x0i.reshape(rows, G, P)], 2).reshape(rows, G * 2 * P)
    d_t = jnp.tile(d_skip, (1, L)).reshape(G, 1, L * H)
    g5 = pl.pallas_call(
        functools.partial(_s5_out_kernel, gpb=gpb),
        grid=(G // gpb,),
        in_specs=[pl.BlockSpec((gpb, rows, L * H), lambda g: (g, 0, 0)),
                  pl.BlockSpec((rows, gpb * 2 * P), lambda g: (0, g)),
                  pl.BlockSpec((gpb, L * H, L * H), lambda g: (g, 0, 0)),
                  pl.BlockSpec((gpb, 2 * P, L * H), lambda g: (g, 0, 0)),
                  pl.BlockSpec((gpb, 1, L * H), lambda g: (g, 0, 0))],
        out_specs=pl.BlockSpec((gpb, rows, L * H), lambda g: (g, 0, 0)),
        out_shape=jax.ShapeDtypeStruct((G, rows, L * H), F32),
        compiler_params=_cparams(("arbitrary",)),
        name="s5_out",
    )(ug, x0, m_op.astype(BF16), cs_op.astype(BF16), d_t)
    g5 = jnp.transpose(g5.reshape(G, nchunk, bsz, L, H), (2, 1, 3, 0, 4)).reshape(bsz, t, G * H)
    return g5, er.reshape(bsz, G, P), ei.reshape(bsz, G, P)


def _s5_step_kernel(u_ref, x0r_ref, x0i_ref, ar_ref, ai_ref, bbr_ref, bbi_ref, cr_ref, ci_ref, d_ref,
                    g_ref, xr_ref, xi_ref, *, nblk, bw, sw):
    for q in range(nblk):
        u = u_ref[:, q * bw:(q + 1) * bw]
        ub = u.astype(BF16)
        sl = slice(q * sw, (q + 1) * sw)
        ar, ai = ar_ref[:, sl], ai_ref[:, sl]
        x0r, x0i = x0r_ref[:, sl], x0i_ref[:, sl]
        xr = ar * x0r - ai * x0i + _dot(ub, bbr_ref[q])
        xi = ar * x0i + ai * x0r + _dot(ub, bbi_ref[q])
        xr_ref[:, sl] = xr
        xi_ref[:, sl] = xi
        y = _dot(xr.astype(BF16), cr_ref[q]) - _dot(xi.astype(BF16), ci_ref[q]) + d_ref[:, q * bw:(q + 1) * bw] * u
        g_ref[:, q * bw:(q + 1) * bw] = _gelu(y)


def _s5_step(u, x0r, x0i, lam_re, lam_im, b_re, b_im, c_re, c_im, log_dt, d_skip):
    n = u.shape[0]
    G, P, H = S5_GROUPS, S5_STATE, S5_GROUP
    gb = 16
    nblk = G // gb
    _, _, ab_re, ab_im, bb_re, bb_im = _s5_discretize(lam_re, lam_im, b_re, b_im, log_dt)
    eye = jnp.eye(gb, dtype=F32)

    def in_blocks(w):
        w4 = w.reshape(nblk, gb, P, H)
        return jnp.einsum('qgph,gk->qghkp', w4, eye).reshape(nblk, gb * H, gb * P)

    def out_blocks(w):
        w4 = w.reshape(nblk, gb, H, P)
        return jnp.einsum('qghp,gk->qgpkh', w4, eye).reshape(nblk, gb * P, gb * H)

    full = lambda i: (0, 0)
    full3 = lambda i: (0, 0, 0)
    g5, xr, xi = pl.pallas_call(
        functools.partial(_s5_step_kernel, nblk=nblk, bw=gb * H, sw=gb * P),
        grid=(1,),
        in_specs=[pl.BlockSpec((n, G * H), full), pl.BlockSpec((n, G * P), full), pl.BlockSpec((n, G * P), full),
                  pl.BlockSpec((1, G * P), full), pl.BlockSpec((1, G * P), full),
                  pl.BlockSpec((nblk, gb * H, gb * P), full3), pl.BlockSpec((nblk, gb * H, gb * P), full3),
                  pl.BlockSpec((nblk, gb * P, gb * H), full3), pl.BlockSpec((nblk, gb * P, gb * H), full3),
                  pl.BlockSpec((1, G * H), full)],
        out_specs=[pl.BlockSpec((n, G * H), full), pl.BlockSpec((n, G * P), full), pl.BlockSpec((n, G * P), full)],
        out_shape=[jax.ShapeDtypeStruct((n, G * H), F32), jax.ShapeDtypeStruct((n, G * P), F32),
                   jax.ShapeDtypeStruct((n, G * P), F32)],
        compiler_params=_cparams(("arbitrary",)),
        name="s5_step",
    )(u, x0r.reshape(n, G * P), x0i.reshape(n, G * P), ab_re.reshape(1, G * P), ab_im.reshape(1, G * P),
      in_blocks(bb_re).astype(BF16), in_blocks(bb_im).astype(BF16),
      out_blocks(c_re).astype(BF16), out_blocks(c_im).astype(BF16), d_skip.reshape(1, G * H))
    return g5, xr.reshape(n, G, P), xi.reshape(n, G, P)


def _sg_kernel(u_ref, v_ref, g_ref, b_ref, w_ref, bias_ref, o_ref, vl_ref, *, cpb):
    i = pl.program_id(0)
    v = _ln_rows(v_ref[...], g_ref[...], b_ref[...])
    vb = v.astype(BF16)
    hd = SG_WIDTH // SG_HEADS
    for h in range(SG_HEADS):
        sl = slice(h * hd, (h + 1) * hd)
        s = _dot(w_ref[h], vb[:, sl]) + bias_ref[:, sl]
        o_ref[:, sl] = u_ref[:, sl] * s

    @pl.when(i % cpb == cpb - 1)
    def _():
        vl_ref[...] = v


def _spatial_gating_prompt(z, col0, bsz, t, ln_g, ln_b, w_s, b_s):
    cpb = t // SG_CHUNK
    cu = col0 // SG_WIDTH
    w = (w_s * jnp.tril(jnp.ones((SG_CHUNK, SG_CHUNK), F32))).astype(BF16)
    bias = jnp.repeat(b_s.T, SG_WIDTH // SG_HEADS, axis=1)
    fix = lambda i: (0, 0)
    return pl.pallas_call(
        functools.partial(_sg_kernel, cpb=cpb),
        grid=(bsz * cpb,),
        in_specs=[pl.BlockSpec((SG_CHUNK, SG_WIDTH), lambda i: (i, cu)),
                  pl.BlockSpec((SG_CHUNK, SG_WIDTH), lambda i: (i, cu + 1)),
                  pl.BlockSpec((1, SG_WIDTH), fix), pl.BlockSpec((1, SG_WIDTH), fix),
                  pl.BlockSpec((SG_HEADS, SG_CHUNK, SG_CHUNK), lambda i: (0, 0, 0)),
                  pl.BlockSpec((SG_CHUNK, SG_WIDTH), fix)],
        out_specs=[pl.BlockSpec((SG_CHUNK, SG_WIDTH), lambda i: (i, 0)),
                   pl.BlockSpec((None, SG_CHUNK, SG_WIDTH), lambda i: (i // cpb, 0, 0))],
        out_shape=[jax.ShapeDtypeStruct((bsz * t, SG_WIDTH), F32),
                   jax.ShapeDtypeStruct((bsz, SG_CHUNK, SG_WIDTH), F32)],
        compiler_params=_cparams(("arbitrary",)),
        name="spatial_gating",
    )(z, z, ln_g.reshape(1, -1), ln_b.reshape(1, -1), w, bias)


def _sg_step_kernel(u_ref, v_ref, g_ref, b_ref, w0_ref, b0_ref, o_ref, vn_ref):
    v = _ln_rows(v_ref[...], g_ref[...], b_ref[...])
    vn_ref[...] = v
    o_ref[...] = u_ref[...] * (w0_ref[...] * v + b0_ref[...])


def _spatial_gating_step(zs, ln_g, ln_b, w_s, b_s):
    n = zs.shape[0]
    hd = SG_WIDTH // SG_HEADS
    w0 = jnp.repeat(w_s[:, 0, 0], hd).reshape(1, SG_WIDTH)
    b0 = jnp.repeat(b_s[:, 0], hd).reshape(1, SG_WIDTH)
    fix = lambda i: (0, 0)
    return pl.pallas_call(
        _sg_step_kernel,
        grid=(1,),
        in_specs=[pl.BlockSpec((n, SG_WIDTH), lambda i: (0, 0)), pl.BlockSpec((n, SG_WIDTH), lambda i: (0, 1)),
                  pl.BlockSpec((1, SG_WIDTH), fix), pl.BlockSpec((1, SG_WIDTH), fix),
                  pl.BlockSpec((1, SG_WIDTH), fix), pl.BlockSpec((1, SG_WIDTH), fix)],
        out_specs=[pl.BlockSpec((n, SG_WIDTH), fix), pl.BlockSpec((n, SG_WIDTH), fix)],
        out_shape=[jax.ShapeDtypeStruct((n, SG_WIDTH), F32)] * 2,
        compiler_params=_cparams(("arbitrary",)),
        name="spatial_gating_step",
    )(zs, zs, ln_g.reshape(1, -1), ln_b.reshape(1, -1), w0, b0)


def _online_softmax_update(s, m_sc, l_sc):
    m_old = m_sc[...]
    m_new = jnp.maximum(m_old, jnp.max(s, -1, keepdims=True))
    a = jnp.exp(m_old - m_new)
    p = jnp.exp(s - m_new)
    l_sc[...] = a * l_sc[...] + jnp.sum(p, -1, keepdims=True)
    m_sc[...] = m_new
    return a, p


def _causal_mask(s, qi, kb, tk):
    row = lax.broadcasted_iota(jnp.int32, s.shape, 0)
    col = lax.broadcasted_iota(jnp.int32, s.shape, 1)
    qpos = qi * Q_BLOCK + (row & (Q_BLOCK - 1))
    return jnp.where(qpos >= kb * tk + col, s, NEG_INF)


def _mla_prefill_kernel(ql_ref, qp_ref, ckv_ref, kpe_ref, wuv_ref, y_ref, m_sc, l_sc, acc_sc, *, tk, scale):
    qi, kb = pl.program_id(1), pl.program_id(2)

    @pl.when(kb == 0)
    def _():
        m_sc[...] = jnp.full(m_sc.shape, NEG_INF, F32)
        l_sc[...] = jnp.zeros(l_sc.shape, F32)
        acc_sc[...] = jnp.zeros(acc_sc.shape, F32)

    @pl.when(kb * tk <= qi * Q_BLOCK + Q_BLOCK - 1)
    def _():
        k1 = ckv_ref[...].astype(BF16)
        s = (_dot_nt(ql_ref[...], k1) + _dot_nt(qp_ref[...], kpe_ref[...].astype(BF16))) * scale
        s = _causal_mask(s, qi, kb, tk)
        a, p = _online_softmax_update(s, m_sc, l_sc)
        acc_sc[...] = a * acc_sc[...] + _dot(p.astype(BF16), k1)

    @pl.when(kb == pl.num_programs(2) - 1)
    def _():
        o = (acc_sc[...] / l_sc[...]).astype(BF16)
        for h in range(MLA_HEADS):
            y_ref[:, h * MLA_V:(h + 1) * MLA_V] = _dot(o[h * Q_BLOCK:(h + 1) * Q_BLOCK],
                                                        wuv_ref[:, h * MLA_V:(h + 1) * MLA_V])


def _mla_prefill(q_lat, q_pe, c_kv, k_pe, w_uv, *, tk=512):
    bsz, t, nh, r = q_lat.shape
    nq = t // Q_BLOCK
    tk = min(tk, t)
    rows = nh * Q_BLOCK

    def blocks(q):
        d = q.shape[-1]
        return jnp.transpose(q.reshape(bsz, nq, Q_BLOCK, nh, d), (0, 1, 3, 2, 4)).reshape(bsz, nq, rows, d).astype(BF16)

    kmap = lambda b, qi, kb: (b, jnp.minimum(kb, (qi * Q_BLOCK + Q_BLOCK - 1) // tk), 0)
    scale = (MLA_NOPE + MLA_ROPE) ** -0.5
    return pl.pallas_call(
        functools.partial(_mla_prefill_kernel, tk=tk, scale=scale),
        grid=(bsz, nq, t // tk),
        in_specs=[pl.BlockSpec((None, None, rows, r), lambda b, qi, kb: (b, qi, 0, 0)),
                  pl.BlockSpec((None, None, rows, MLA_ROPE), lambda b, qi, kb: (b, qi, 0, 0)),
                  pl.BlockSpec((None, tk, r), kmap),
                  pl.BlockSpec((None, tk, MLA_ROPE), kmap),
                  pl.BlockSpec((r, nh * MLA_V), lambda b, qi, kb: (0, 0))],
        out_specs=pl.BlockSpec((None, Q_BLOCK, nh * MLA_V), lambda b, qi, kb: (b, qi, 0)),
        out_shape=jax.ShapeDtypeStruct((bsz, t, nh * MLA_V), F32),
        scratch_shapes=[pltpu.VMEM((rows, 1), F32), pltpu.VMEM((rows, 1), F32), pltpu.VMEM((rows, r), F32)],
        compiler_params=_cparams(("arbitrary", "arbitrary", "arbitrary")),
        name="mla_prefill",
    )(blocks(q_lat), blocks(q_pe), c_kv, k_pe, w_uv.reshape(r, nh * MLA_V).astype(BF16))


def _diff_finalize(o0, o1, lam, subln, lam_init):
    o = o0 - lam * o1
    ms = jnp.mean(o * o, -1, keepdims=True)
    return o * lax.rsqrt(ms + 1e-5) * subln * (1.0 - lam_init)


def _diff_prefill_kernel(q_ref, k_ref, v_ref, lam_ref, sub_ref, y_ref, m_sc, l_sc, acc_sc, *, tk, scale, lam_init):
    qi, kb = pl.program_id(1), pl.program_id(2)

    @pl.when(kb == 0)
    def _():
        m_sc[...] = jnp.full(m_sc.shape, NEG_INF, F32)
        l_sc[...] = jnp.zeros(l_sc.shape, F32)
        acc_sc[...] = jnp.zeros(acc_sc.shape, F32)

    @pl.when(kb * tk <= qi * Q_BLOCK + Q_BLOCK - 1)
    def _():
        vb = v_ref[...].astype(BF16)
        for mi in range(2):
            s = _dot_nt(q_ref[mi], k_ref[mi].astype(BF16)) * scale
            s = _causal_mask(s, qi, kb, tk)
            a, p = _online_softmax_update(s, m_sc.at[mi], l_sc.at[mi])
            acc_sc[mi] = a * acc_sc[mi] + _dot(p.astype(BF16), vb)

    @pl.when(kb == pl.num_programs(2) - 1)
    def _():
        o = _diff_finalize(acc_sc[0] / l_sc[0], acc_sc[1] / l_sc[1], lam_ref[...], sub_ref[...], lam_init)
        for g in range(DIFF_GROUP):
            y_ref[:, g * DIFF_V_DIM:(g + 1) * DIFF_V_DIM] = o[g * Q_BLOCK:(g + 1) * Q_BLOCK]


def _diff_prefill(dq, dk, dv, lam, subln, lam_init, *, tk=512):
    bsz, t = dq.shape[:2]
    nq = t // Q_BLOCK
    tk = min(tk, t)
    rows = DIFF_GROUP * Q_BLOCK
    d = DIFF_HEAD_DIM
    q = jnp.transpose(dq.reshape(bsz, nq, Q_BLOCK, DIFF_KV_HEADS, DIFF_GROUP, 2, d), (0, 3, 1, 5, 4, 2, 6))
    q = q.reshape(bsz * DIFF_KV_HEADS, nq, 2, rows, d).astype(BF16)
    k = jnp.transpose(dk, (0, 2, 3, 1, 4)).reshape(bsz * DIFF_KV_HEADS, 2, t, d)
    v = jnp.transpose(dv, (0, 2, 1, 3)).reshape(bsz * DIFF_KV_HEADS, t, DIFF_V_DIM)
    last = lambda qi: (qi * Q_BLOCK + Q_BLOCK - 1) // tk
    width = DIFF_GROUP * DIFF_V_DIM
    return pl.pallas_call(
        functools.partial(_diff_prefill_kernel, tk=tk, scale=d ** -0.5, lam_init=lam_init),
        grid=(bsz * DIFF_KV_HEADS, nq, t // tk),
        in_specs=[pl.BlockSpec((None, None, 2, rows, d), lambda a, qi, kb: (a, qi, 0, 0, 0)),
                  pl.BlockSpec((None, 2, tk, d), lambda a, qi, kb: (a, 0, jnp.minimum(kb, last(qi)), 0)),
                  pl.BlockSpec((None, tk, DIFF_V_DIM), lambda a, qi, kb: (a, jnp.minimum(kb, last(qi)), 0)),
                  pl.BlockSpec((1, DIFF_V_DIM), lambda a, qi, kb: (0, 0)),
                  pl.BlockSpec((1, DIFF_V_DIM), lambda a, qi, kb: (0, 0))],
        out_specs=pl.BlockSpec((None, Q_BLOCK, width),
                               lambda a, qi, kb: (a // DIFF_KV_HEADS, qi, a % DIFF_KV_HEADS)),
        out_shape=jax.ShapeDtypeStruct((bsz, t, DIFF_KV_HEADS * width), F32),
        scratch_shapes=[pltpu.VMEM((2, rows, 1), F32), pltpu.VMEM((2, rows, 1), F32),
                        pltpu.VMEM((2, rows, DIFF_V_DIM), F32)],
        compiler_params=_cparams(("arbitrary", "arbitrary", "arbitrary")),
        name="diff_prefill",
    )(q, k, v, jnp.full((1, DIFF_V_DIM), lam, F32), subln.reshape(1, DIFF_V_DIM))


def _decode_init(s0_ref, v0_ref, m_sc, l_sc, acc_sc):
    m_sc[...] = s0_ref[...]
    l_sc[...] = jnp.ones(l_sc.shape, F32)
    acc_sc[...] = jnp.broadcast_to(v0_ref[...], acc_sc.shape)


def _mla_decode_kernel(pt_ref, *refs, npg, scale):
    del pt_ref
    q1_ref, q2_ref, s0_ref, v0_ref = refs[:4]
    lat_refs = refs[4:4 + npg]
    pet_refs = refs[4 + npg:4 + 2 * npg]
    o_ref, m_sc, l_sc, acc_sc = refs[4 + 2 * npg:]
    step = pl.program_id(1)

    @pl.when(step == 0)
    def _():
        _decode_init(s0_ref, v0_ref, m_sc, l_sc, acc_sc)

    q1, q2 = q1_ref[...], q2_ref[...]
    pages, scores = [], []
    for i in range(npg):
        lat = lat_refs[i][...].astype(BF16)
        scores.append(_dot_nt(q1, lat) + _dot(q2, pet_refs[i][...].astype(BF16)))
        pages.append(lat)
    s = jnp.concatenate(scores, axis=1) * scale
    a, p = _online_softmax_update(s, m_sc, l_sc)
    pb = p.astype(BF16)
    pw = scores[0].shape[1]
    acc = a * acc_sc[...]
    for i in range(npg):
        acc = acc + _dot(pb[:, i * pw:(i + 1) * pw], pages[i])
    acc_sc[...] = acc

    @pl.when(step == pl.num_programs(1) - 1)
    def _():
        o_ref[...] = acc_sc[...] / l_sc[...]


def _diff_decode_kernel(pt_ref, *refs, npg, page, scale):
    del pt_ref
    q_ref, s0_ref, v0_ref = refs[:3]
    kt_refs = refs[3:3 + npg]
    v_refs = refs[3 + npg:3 + 2 * npg]
    o_ref, m_sc, l_sc, acc_sc = refs[3 + 2 * npg:]
    step = pl.program_id(1)

    @pl.when(step == 0)
    def _():
        _decode_init(s0_ref, v0_ref, m_sc, l_sc, acc_sc)

    q = q_ref[...]
    scores = [_dot(q, kt_refs[i][...].astype(BF16)) for i in range(npg)]
    s = jnp.concatenate(scores, axis=1) * scale
    a, p = _online_softmax_update(s, m_sc, l_sc)
    pb = p.astype(BF16)
    for h in range(DIFF_KV_HEADS):
        sl = slice(h * DIFF_V_DIM, (h + 1) * DIFF_V_DIM)
        acc = a * acc_sc[:, sl]
        for i in range(npg):
            vh = v_refs[i][pl.ds(h, page, stride=DIFF_KV_HEADS), :].astype(BF16)
            acc = acc + _dot(pb[:, i * page:(i + 1) * page], vh)
        acc_sc[:, sl] = acc

    @pl.when(step == pl.num_programs(1) - 1)
    def _():
        o_ref[...] = acc_sc[...] / l_sc[...]


def _paged_call(kernel_fn, name, page_table, small, pools, r, dv):
    bsz, npages = page_table.shape
    npg = math.gcd(PAGES_PER_STEP, npages)
    per_b = lambda b, s, pt: (b, 0, 0)
    in_specs = [pl.BlockSpec((None,) + a.shape[1:], per_b) for a in small]
    args = list(small)
    for pool in pools:
        in_specs += [pl.BlockSpec((None,) + pool.shape[1:],
                                  functools.partial(lambda b, s, pt, i: (pt[b, s * npg + i], 0, 0), i=i))
                     for i in range(npg)]
        args += [pool] * npg
    grid_spec = pltpu.PrefetchScalarGridSpec(
        num_scalar_prefetch=1,
        grid=(bsz, npages // npg),
        in_specs=in_specs,
        out_specs=pl.BlockSpec((None, r, dv), per_b),
        scratch_shapes=[pltpu.VMEM((r, 1), F32), pltpu.VMEM((r, 1), F32), pltpu.VMEM((r, dv), F32)],
    )
    return pl.pallas_call(
        functools.partial(kernel_fn, npg=npg),
        grid_spec=grid_spec,
        out_shape=jax.ShapeDtypeStruct((bsz, r, dv), F32),
        compiler_params=_cparams(("arbitrary", "arbitrary")),
        name=name,
    )(page_table, *args)


def _moe_kernel(first_ref, e_ref, nsub_ref, oblk_ref, *refs):
    del first_ref, e_ref, oblk_ref
    x_refs = refs[:MOE_NSUB]
    wg_ref, bg_ref, wu_ref, bu_ref, wd_ref, bd_ref, o_ref, wg_sc, wu_sc, wd_sc = refs[MOE_NSUB:]
    s, j = pl.program_id(0), pl.program_id(1)
    ns = nsub_ref[s]

    @pl.when(ns > 0)
    def _():
        @pl.when(j == 0)
        def _():
            o_ref[...] = jnp.broadcast_to(bd_ref[...], o_ref.shape)

        wg_sc[...] = wg_ref[...].astype(BF16)
        wu_sc[...] = wu_ref[...].astype(BF16)
        wd_sc[...] = wd_ref[...].astype(BF16)
        for i in range(MOE_NSUB):
            @pl.when(i < ns)
            def _(i=i):
                xb = x_refs[i][...].astype(BF16)
                g = jnp.minimum(_dot(xb, wg_sc[...]) + bg_ref[...], SWIGLU_LIMIT)
                u = jnp.clip(_dot(xb, wu_sc[...]) + bu_ref[...], -SWIGLU_LIMIT, SWIGLU_LIMIT)
                h = (u + 1.0) * (g * _sigmoid(SWIGLU_ALPHA * g))
                o_ref[i * MOE_SUB:(i + 1) * MOE_SUB, :] += _dot(h.astype(BF16), wd_sc[...])


def _moe(x, logits, layer, wg, bg, wu, bu, wd, bd):
    n, d = x.shape
    f = wg.shape[-1]
    sub, nsub_max, tf = MOE_SUB, MOE_NSUB, MOE_TF
    tm = sub * nsub_max
    nf = f // tf
    nk = n * TOP_K
    nblk = nk // sub + N_EXPERTS
    nsb = (nblk + (nsub_max - 1) * N_EXPERTS) // nsub_max
    top_v, top_i = lax.top_k(logits, TOP_K)
    gates = jax.nn.softmax(top_v, axis=-1)
    flat_e = top_i.reshape(-1).astype(I32)
    order = jnp.argsort(flat_e).astype(I32)
    inv = jnp.argsort(order).astype(I32)
    eids = jnp.arange(N_EXPERTS, dtype=I32)
    counts = jnp.sum((flat_e[:, None] == eids[None, :]).astype(I32), axis=0)
    grp_start = (jnp.cumsum(counts) - counts).astype(I32)
    blocks_e = (counts + sub - 1) // sub
    blk_end = jnp.cumsum(blocks_e).astype(I32)
    blk_start = blk_end - blocks_e
    sb_per_e = (blocks_e + nsub_max - 1) // nsub_max
    sb_end = jnp.cumsum(sb_per_e).astype(I32)
    sb_start = sb_end - sb_per_e
    r = jnp.arange(nblk * sub, dtype=I32)
    e_r = jnp.minimum(jnp.sum((r[:, None] // sub >= blk_end[None, :]).astype(I32), axis=1), N_EXPERTS - 1)
    rank_r = r - blk_start[e_r] * sub
    valid_r = (r // sub < blk_end[-1]) & (rank_r < counts[e_r])
    tok = jnp.where(valid_r, order[jnp.clip(grp_start[e_r] + rank_r, 0, nk - 1)] // TOP_K, 0)
    xs = x[tok]
    n_used = sb_end[-1]
    sb_ids = jnp.arange(nsb, dtype=I32)
    sbc = jnp.minimum(sb_ids, n_used - 1)
    sb_e = jnp.minimum(jnp.sum((sbc[:, None] >= sb_end[None, :]).astype(I32), axis=1), N_EXPERTS - 1)
    kk = sbc - sb_start[sb_e]
    first = blk_start[sb_e] + kk * nsub_max
    nsub = jnp.where(sb_ids < n_used, jnp.clip(blocks_e[sb_e] - kk * nsub_max, 0, nsub_max), 0).astype(I32)

    jj = lambda s, j, ns: jnp.where(ns[s] > 0, j, nf - 1)

    def x_spec(i):
        return pl.BlockSpec((sub, d), lambda s, j, fr, e, ns, ob: (jnp.minimum(fr[s] + i, nblk - 1), 0))

    grid_spec = pltpu.PrefetchScalarGridSpec(
        num_scalar_prefetch=4,
        grid=(nsb, nf),
        in_specs=[x_spec(i) for i in range(nsub_max)] + [
            pl.BlockSpec((None, None, d, tf), lambda s, j, fr, e, ns, ob: (layer, e[s], 0, jj(s, j, ns))),
            pl.BlockSpec((None, None, 1, tf), lambda s, j, fr, e, ns, ob: (layer, e[s], 0, jj(s, j, ns))),
            pl.BlockSpec((None, None, d, tf), lambda s, j, fr, e, ns, ob: (layer, e[s], 0, jj(s, j, ns))),
            pl.BlockSpec((None, None, 1, tf), lambda s, j, fr, e, ns, ob: (layer, e[s], 0, jj(s, j, ns))),
            pl.BlockSpec((None, None, tf, d), lambda s, j, fr, e, ns, ob: (layer, e[s], jj(s, j, ns), 0)),
            pl.BlockSpec((None, None, 1, d), lambda s, j, fr, e, ns, ob: (layer, e[s], 0, 0))],
        out_specs=pl.BlockSpec((tm, d), lambda s, j, fr, e, ns, ob: (ob[s], 0)),
        scratch_shapes=[pltpu.VMEM((d, tf), BF16), pltpu.VMEM((d, tf), BF16), pltpu.VMEM((tf, d), BF16)],
    )
    nl, ne = wg.shape[:2]
    yb = pl.pallas_call(
        _moe_kernel,
        grid_spec=grid_spec,
        out_shape=jax.ShapeDtypeStruct((nsb * tm, d), F32),
        compiler_params=_cparams(("arbitrary", "arbitrary")),
        name="moe_experts",
    )(first.astype(I32), sb_e.astype(I32), nsub, sbc.astype(I32), *([xs] * nsub_max),
      wg, bg.reshape(nl, ne, 1, f), wu, bu.reshape(nl, ne, 1, f), wd, bd.reshape(nl, ne, 1, d))
    rank_a = inv - grp_start[flat_e]
    dest = (sb_start[flat_e] + rank_a // tm) * tm + rank_a % tm
    return yb[dest.reshape(n, TOP_K).T.reshape(-1)], gates


def _rmsnorm(x, g, eps=1e-6):
    return x * lax.rsqrt(jnp.mean(x * x, -1, keepdims=True) + eps) * g


def _rope(x, pos):
    d = x.shape[-1]
    inv = ROPE_THETA ** (-jnp.arange(0, d, 2, dtype=F32) / d)
    ang = pos.astype(F32)[:, None] * inv[None, :]
    ang = ang.reshape((ang.shape[0],) + (1,) * (x.ndim - 2) + (ang.shape[1],))
    cos, sin = jnp.cos(ang), jnp.sin(ang)
    x1, x2 = jnp.split(x, 2, axis=-1)
    return jnp.concatenate([x1 * cos - x2 * sin, x1 * sin + x2 * cos], -1)


def _layer_ab(x, np_rows, bsz, t, state_re, state_im, w):
    (w_in, lam_re, lam_im, b_re, b_im, c_re, c_im, d_skip, log_dt, w_glu, b_glu,
     ln_g, ln_b, w_s, b_s) = w
    z = _mm(x, w_in.astype(BF16), tm=640, tn=512, gelu_from=S5_WIDTH // 512)
    ops = _s5_chunk_operators(lam_re, lam_im, b_re, b_im, c_re, c_im, log_dt)
    g5_p, pr, pi = _s5_prompt(z[:np_rows, :S5_WIDTH].reshape(bsz, t, S5_WIDTH), ops, d_skip)
    g5_s, sr, si = _s5_step(z[np_rows:, :S5_WIDTH], state_re, state_im,
                            lam_re, lam_im, b_re, b_im, c_re, c_im, log_dt, d_skip)
    g5 = jnp.concatenate([g5_p.reshape(np_rows, S5_WIDTH), g5_s], 0)
    o5 = _glu(g5, w_glu.astype(BF16), b_glu.reshape(1, -1), tm=640, tn=512)
    osg_p, v_p = _spatial_gating_prompt(z, S5_WIDTH, bsz, t, ln_g, ln_b, w_s, b_s)
    osg_s, v_s = _spatial_gating_step(z[np_rows:, S5_WIDTH:], ln_g, ln_b, w_s, b_s)
    osg = jnp.concatenate([osg_p, osg_s], 0)
    return o5, osg, (pr, pi, sr, si, v_p, v_s[:, None, :])


def _layer_cd(x, np_rows, bsz, t, pos, layer, caches, page_table, w):
    (w_in, q_norm, kv_norm, w_uq, w_uk, w_uv, lq1, lk1, lq2, lk2, subln) = w
    lat_pool, pet_pool, dkt_pool, dv_pool = caches
    n = x.shape[0]
    ns = n - np_rows
    nh = MLA_HEADS
    page = lat_pool.shape[1]
    a1, a2 = MLA_Q_RANK + MLA_KV_RANK, MLA_Q_RANK + MLA_KV_RANK + MLA_ROPE
    ndq = DIFF_HEADS * 2 * DIFF_HEAD_DIM
    ndk = DIFF_KV_HEADS * 2 * DIFF_HEAD_DIM
    ndv = DIFF_KV_HEADS * DIFF_V_DIM
    w_perm = jnp.concatenate([w_in[:, :a1], w_in[:, a2:], w_in[:, a1:a2],
                              jnp.zeros((w_in.shape[0], 128 - MLA_ROPE), F32)], 1).astype(BF16)
    z = _mm(x, w_perm, tm=640, tn=w_perm.shape[1])
    c = 0
    q_lat = z[:, c:c + MLA_Q_RANK]; c += MLA_Q_RANK
    kv_lat = z[:, c:c + MLA_KV_RANK]; c += MLA_KV_RANK
    dq = z[:, c:c + ndq]; c += ndq
    dk = z[:, c:c + ndk]; c += ndk
    dv = z[:, c:c + ndv]; c += ndv
    k_pe = z[:, c:c + MLA_ROPE]
    q = _mm(_rmsnorm(q_lat, q_norm), w_uq.reshape(MLA_Q_RANK, -1).astype(BF16), tm=640, tn=512)
    q = q.reshape(n, nh, MLA_NOPE + MLA_ROPE)
    q_pe = _rope(q[..., MLA_NOPE:], pos)
    q_abs = _bmm(jnp.transpose(q[..., :MLA_NOPE], (1, 0, 2)),
                 jnp.transpose(w_uk, (1, 2, 0)).astype(BF16), tm=640)
    q_abs = jnp.transpose(q_abs, (1, 0, 2))
    c_kv = _rmsnorm(kv_lat, kv_norm)
    k_pe = _rope(k_pe, pos)
    dq = _rope(dq.reshape(n, DIFF_KV_HEADS, DIFF_GROUP, 2, DIFF_HEAD_DIM), pos)
    dk = _rope(dk.reshape(n, DIFF_KV_HEADS, 2, DIFF_HEAD_DIM), pos)
    dv = dv.reshape(n, DIFF_KV_HEADS, DIFF_V_DIM)
    lam_init = 0.8 - 0.6 * math.exp(-0.3 * layer)
    lam = jnp.exp(jnp.sum(lq1 * lk1)) - jnp.exp(jnp.sum(lq2 * lk2)) + lam_init

    pr = lambda a: a[:np_rows].reshape((bsz, t) + a.shape[1:])
    y_mla_p = _mla_prefill(pr(q_abs), pr(q_pe), pr(c_kv), pr(k_pe), w_uv)
    y_diff_p = _diff_prefill(pr(dq), pr(dk), pr(dv), lam, subln, lam_init)

    sm = lambda a: a[np_rows:]
    mla_scale = (MLA_NOPE + MLA_ROPE) ** -0.5
    s0 = (jnp.einsum('bhr,br->bh', sm(q_abs), sm(c_kv)) + jnp.einsum('bhd,bd->bh', sm(q_pe), sm(k_pe))) * mla_scale
    rpad = DECODE_ROWS - nh
    padr = lambda a: jnp.pad(a, ((0, 0), (0, rpad)) + ((0, 0),) * (a.ndim - 2))
    o_mla = _paged_call(functools.partial(_mla_decode_kernel, scale=mla_scale), "mla_decode", page_table,
                        [padr(sm(q_abs)).astype(BF16), padr(sm(q_pe)).astype(BF16), padr(s0)[..., None],
                         sm(c_kv)[:, None, :]],
                        [lat_pool, pet_pool], DECODE_ROWS, MLA_KV_RANK)[:, :nh]
    y_mla_s = _bmm(jnp.transpose(o_mla, (1, 0, 2)), jnp.transpose(w_uv, (1, 0, 2)).astype(BF16), tm=ns)
    y_mla_s = jnp.transpose(y_mla_s, (1, 0, 2)).reshape(ns, nh * MLA_V)

    diff_scale = DIFF_HEAD_DIM ** -0.5
    dq_s, dk_s, dv_s = sm(dq), sm(dk), sm(dv)
    qrow = jnp.transpose(dq_s, (0, 1, 3, 2, 4))
    sel = jnp.eye(DIFF_KV_HEADS * 2, dtype=F32).reshape(DIFF_KV_HEADS, 2, 1, DIFF_KV_HEADS * 2, 1)
    qblk = (qrow[:, :, :, :, None, :] * sel[None]).reshape(ns, DECODE_ROWS, DIFF_KV_HEADS * 2 * DIFF_HEAD_DIM)
    s0d = jnp.einsum('brk,bk->br', qblk, dk_s.reshape(ns, -1)) * diff_scale
    o_d = _paged_call(functools.partial(_diff_decode_kernel, page=page, scale=diff_scale), "diff_decode",
                      page_table, [qblk.astype(BF16), s0d[..., None], dv_s.reshape(ns, 1, -1)],
                      [dkt_pool, dv_pool], DECODE_ROWS, DIFF_KV_HEADS * DIFF_V_DIM)
    o_d = o_d.reshape(ns, DIFF_KV_HEADS, 2, DIFF_GROUP, DIFF_KV_HEADS, DIFF_V_DIM)
    o_d = jnp.stack([o_d[:, h, :, :, h] for h in range(DIFF_KV_HEADS)], 1)
    y_diff_s = _diff_finalize(o_d[:, :, 0], o_d[:, :, 1], lam, subln, lam_init).reshape(ns, -1)

    y_mla = jnp.concatenate([y_mla_p.reshape(np_rows, -1), y_mla_s], 0)
    y_diff = jnp.concatenate([y_diff_p.reshape(np_rows, -1), y_diff_s], 0)
    outs = (pr(c_kv), pr(k_pe), sm(c_kv)[:, None], sm(k_pe)[:, None],
            pr(dk), pr(dv), sm(dk)[:, None], sm(dv)[:, None])
    return y_mla, y_diff, outs


def kernel(x_prompt, x_sample, state_s5_re, state_s5_im, cache_mla_latent, cache_mla_krope, cache_diff_k, cache_diff_v, page_table, p_prompt, p_sample, ab_w_in, s5_lambda_re, s5_lambda_im, s5_b_re, s5_b_im, s5_c_re, s5_c_im, s5_d, s5_log_dt, s5_w_glu, s5_b_glu, sg_ln_g, sg_ln_b, sg_w_s, sg_b_s, ab_w_out, cd_w_in, mla_q_norm, mla_kv_norm, mla_w_uq, mla_w_uk, mla_w_uv, diff_lq1, diff_lk1, diff_lq2, diff_lk2, diff_subln, cd_w_out, ln1_g, ln1_b, ln2_g, ln2_b, router_w, router_b, ex_w_gate, ex_b_gate, ex_w_up, ex_b_up, ex_w_down, ex_b_down, ple_w_proj, ple_w_gate):
    bsz, t, d = x_prompt.shape
    nb, ts, _ = x_sample.shape
    assert ts == 1
    np_rows = bsz * t
    npool, page = cache_mla_latent.shape[1:3]
    past_len = page_table.shape[1] * page
    pos = jnp.concatenate([jnp.tile(jnp.arange(t, dtype=jnp.int32), bsz),
                           jnp.full((nb,), past_len, jnp.int32)])
    x = jnp.concatenate([x_prompt.reshape(np_rows, d), x_sample.reshape(nb, d)], 0)
    p_all = jnp.concatenate([p_prompt.reshape(DEPTH, np_rows, -1), p_sample.reshape(DEPTH, nb, -1)], 1)
    ab_outs, cd_outs = [], []
    for i in range(DEPTH):
        j = i // 2
        if i % 2 == 0:
            w = (ab_w_in[j], s5_lambda_re[j], s5_lambda_im[j], s5_b_re[j], s5_b_im[j], s5_c_re[j], s5_c_im[j],
                 s5_d[j], s5_log_dt[j], s5_w_glu[j], s5_b_glu[j], sg_ln_g[j], sg_ln_b[j], sg_w_s[j], sg_b_s[j])
            a, b, outs = _layer_ab(x, np_rows, bsz, t, state_s5_re[j], state_s5_im[j], w)
            ab_outs.append(outs)
            w_out = ab_w_out[j]
        else:
            w = (cd_w_in[j], mla_q_norm[j], mla_kv_norm[j], mla_w_uq[j], mla_w_uk[j], mla_w_uv[j],
                 diff_lq1[j], diff_lk1[j], diff_lq2[j], diff_lk2[j], diff_subln[j])
            caches = (cache_mla_latent[j].reshape(npool, page, -1),
                      jnp.swapaxes(cache_mla_krope[j], 1, 2),
                      jnp.transpose(cache_diff_k[j], (0, 2, 3, 4, 1)).reshape(npool, -1, page),
                      cache_diff_v[j].reshape(npool, page * DIFF_KV_HEADS, DIFF_V_DIM))
            a, b, outs = _layer_cd(x, np_rows, bsz, t, pos, i, caches, page_table, w)
            cd_outs.append(outs)
            w_out = cd_w_out[j]
        ka = a.shape[1]
        x1, logits = _mix_ln_router(x, a, b, w_out[:ka].astype(BF16), w_out[ka:].astype(BF16),
                                    ln1_g[i], ln1_b[i], router_w[i], router_b[i], tm=520)
        ysel, gates = _moe(x1, logits, i, ex_w_gate, ex_b_gate, ex_w_up, ex_b_up, ex_w_down, ex_b_down)
        x = _post(x1, ysel, gates, p_all[i], ple_w_proj[i].astype(BF16), ple_w_gate[i].astype(BF16),
                  ln2_g[i], ln2_b[i], tm=208)
    y_prompt = x[:np_rows].reshape(bsz, t, d)
    y_sample = x[np_rows:].reshape(nb, ts, d)
    st = lambda k: jnp.stack([o[k] for o in ab_outs])
    sc = lambda k: jnp.stack([o[k] for o in cd_outs])
    return (y_prompt, y_sample, st(0), st(1), st(2), st(3), st(4), st(5),
            sc(0), sc(1), sc(2), sc(3), sc(4), sc(5), sc(6), sc(7))
```

```python
import functools
import math

import jax
import jax.numpy as jnp
from jax import lax
from jax.experimental import pallas as pl
from jax.experimental.pallas import tpu as pltpu

F32 = jnp.float32
BF16 = jnp.bfloat16
I32 = jnp.int32

D_MODEL = 2048
DEPTH = 2
S5_WIDTH = 1024
S5_GROUP = 16
S5_GROUPS = 64
S5_STATE = 64
S5_CHUNK = 16
SG_HEADS = 8
SG_WIDTH = 1024
SG_CHUNK = 128
MLA_HEADS = 8
MLA_Q_RANK = 512
MLA_KV_RANK = 256
MLA_NOPE = 128
MLA_ROPE = 64
MLA_V = 128
DIFF_HEADS = 8
DIFF_KV_HEADS = 2
DIFF_GROUP = 4
DIFF_HEAD_DIM = 64
DIFF_V_DIM = 128
N_EXPERTS = 32
TOP_K = 4
SWIGLU_LIMIT = 7.0
SWIGLU_ALPHA = 1.702
PLE_DIM = 256
ROPE_THETA = 10000.0
Q_BLOCK = 128
ALPHA = (2 * DEPTH) ** 0.25
NEG_INF = -1e30

MOE_SUB = 256
MOE_NSUB = 4
MOE_TF = 256
PAGES_PER_STEP = 32
DECODE_ROWS = 16
VMEM_LIMIT = 56 * 1024 * 1024


def _cparams(sem):
    return pltpu.CompilerParams(dimension_semantics=sem, vmem_limit_bytes=VMEM_LIMIT)


def _row_tile(rows, target):
    best = 8
    for cand in range(8, min(rows, target) + 1, 8):
        if rows % cand == 0:
            best = cand
    return best


def _gelu(x):
    return 0.5 * x * (1.0 + jnp.tanh(0.7978845608028654 * (x + 0.044715 * (x * x * x))))


def _sigmoid(x):
    return 1.0 / (1.0 + jnp.exp(-x))


def _dot(a, b):
    return jnp.dot(a, b, preferred_element_type=F32)


def _dot_nt(a, b):
    return lax.dot_general(a, b, (((1,), (1,)), ((), ())), preferred_element_type=F32)


def _ln_rows(x, g, b, eps=1e-5):
    mu = jnp.mean(x, -1, keepdims=True)
    xc = x - mu
    var = jnp.mean(xc * xc, -1, keepdims=True)
    return xc * lax.rsqrt(var + eps) * g + b


def _mm_kernel(x_ref, w_ref, o_ref, *, gelu_from):
    acc = _dot(x_ref[...].astype(BF16), w_ref[...])
    if gelu_from is None:
        o_ref[...] = acc
    else:
        j = pl.program_id(0)

        @pl.when(j < gelu_from)
        def _():
            o_ref[...] = acc

        @pl.when(j >= gelu_from)
        def _():
            o_ref[...] = _gelu(acc)


def _mm(x, w, *, tm, tn, gelu_from=None):
    m, k = x.shape
    n = w.shape[1]
    tm = _row_tile(m, tm)
    return pl.pallas_call(
        functools.partial(_mm_kernel, gelu_from=gelu_from),
        grid=(n // tn, m // tm),
        in_specs=[pl.BlockSpec((tm, k), lambda j, i: (i, 0)),
                  pl.BlockSpec((k, tn), lambda j, i: (0, j))],
        out_specs=pl.BlockSpec((tm, tn), lambda j, i: (i, j)),
        out_shape=jax.ShapeDtypeStruct((m, n), F32),
        compiler_params=_cparams(("arbitrary", "arbitrary")),
        name="mm",
    )(x, w)


def _bmm_kernel(x_ref, w_ref, o_ref):
    o_ref[...] = _dot(x_ref[...].astype(BF16), w_ref[...])


def _bmm(x, w, *, tm):
    h, m, k = x.shape
    n = w.shape[2]
    tm = _row_tile(m, tm)
    return pl.pallas_call(
        _bmm_kernel,
        grid=(h, m // tm),
        in_specs=[pl.BlockSpec((None, tm, k), lambda a, i: (a, i, 0)),
                  pl.BlockSpec((None, k, n), lambda a, i: (a, 0, 0))],
        out_specs=pl.BlockSpec((None, tm, n), lambda a, i: (a, i, 0)),
        out_shape=jax.ShapeDtypeStruct((h, m, n), F32),
        compiler_params=_cparams(("arbitrary", "arbitrary")),
        name="bmm",
    )(x, w)


def _glu_kernel(g_ref, gcol_ref, w_ref, b_ref, o_ref):
    acc = _dot(g_ref[...].astype(BF16), w_ref[...]) + b_ref[...]
    o_ref[...] = gcol_ref[...] * _sigmoid(acc)


def _glu(g5, w, b, *, tm, tn):
    m, k = g5.shape
    n = w.shape[1]
    tm = _row_tile(m, tm)
    return pl.pallas_call(
        _glu_kernel,
        grid=(n // tn, m // tm),
        in_specs=[pl.BlockSpec((tm, k), lambda j, i: (i, 0)),
                  pl.BlockSpec((tm, tn), lambda j, i: (i, j)),
                  pl.BlockSpec((k, tn), lambda j, i: (0, j)),
                  pl.BlockSpec((1, tn), lambda j, i: (0, j))],
        out_specs=pl.BlockSpec((tm, tn), lambda j, i: (i, j)),
        out_shape=jax.ShapeDtypeStruct((m, n), F32),
        compiler_params=_cparams(("arbitrary", "arbitrary")),
        name="glu",
    )(g5, g5, w, b)


def _mix_ln_router_kernel(r_ref, a_ref, b_ref, wa_ref, wb_ref, g_ref, beta_ref,
                          wrh_ref, wrl_ref, br_ref, x_ref, lg_ref):
    mix = _dot(a_ref[...].astype(BF16), wa_ref[...]) + _dot(b_ref[...].astype(BF16), wb_ref[...])
    x = _ln_rows(ALPHA * r_ref[...] + mix, g_ref[...], beta_ref[...])
    x_ref[...] = x
    xh = x.astype(BF16)
    xl = (x - xh.astype(F32)).astype(BF16)
    lg_ref[...] = (_dot(xh, wrh_ref[...]) + _dot(xl, wrh_ref[...]) + _dot(xh, wrl_ref[...])
                   + br_ref[...])


def _mix_ln_router(r, a, b, wa, wb, g, beta, wr, br, *, tm):
    m, d = r.shape
    tm = _row_tile(m, tm)
    ka, kb = a.shape[1], b.shape[1]
    ne = wr.shape[1]
    wr_p = jnp.zeros((d, 128), F32).at[:, :ne].set(wr)
    wrh = wr_p.astype(BF16)
    wrl = (wr_p - wrh.astype(F32)).astype(BF16)
    br_p = jnp.zeros((1, 128), F32).at[0, :ne].set(br)
    row = lambda i: (i, 0)
    fix = lambda i: (0, 0)
    x, lg = pl.pallas_call(
        _mix_ln_router_kernel,
        grid=(m // tm,),
        in_specs=[pl.BlockSpec((tm, d), row), pl.BlockSpec((tm, ka), row), pl.BlockSpec((tm, kb), row),
                  pl.BlockSpec((ka, d), fix), pl.BlockSpec((kb, d), fix),
                  pl.BlockSpec((1, d), fix), pl.BlockSpec((1, d), fix),
                  pl.BlockSpec((d, 128), fix), pl.BlockSpec((d, 128), fix), pl.BlockSpec((1, 128), fix)],
        out_specs=[pl.BlockSpec((tm, d), row), pl.BlockSpec((tm, 128), row)],
        out_shape=[jax.ShapeDtypeStruct((m, d), F32), jax.ShapeDtypeStruct((m, 128), F32)],
        compiler_params=_cparams(("arbitrary",)),
        name="mix_ln_router",
    )(r, a, b, wa, wb, g.reshape(1, d), beta.reshape(1, d), wrh, wrl, br_p)
    return x, lg[:, :ne]


def _post_kernel(x1_ref, *refs):
    ys_refs = refs[:TOP_K]
    gt_ref, p_ref, wp_ref, wg_ref, g_ref, beta_ref, o_ref = refs[TOP_K:]
    y = gt_ref[:, 0:1] * ys_refs[0][...]
    for k in range(1, TOP_K):
        y = y + gt_ref[:, k:k + 1] * ys_refs[k][...]
    x = _ln_rows(ALPHA * x1_ref[...] + y, g_ref[...], beta_ref[...])
    proj = _dot(p_ref[...].astype(BF16), wp_ref[...])
    gate = _sigmoid(_dot(x.astype(BF16), wg_ref[...]))
    o_ref[...] = x + proj * gate


def _post(x1, ysel, gates, p, wp, wg, g, beta, *, tm):
    m, d = x1.shape
    tm = _row_tile(m, tm)
    nb = m // tm
    kp = p.shape[1]
    gt = jnp.zeros((m, 128), F32).at[:, :TOP_K].set(gates)
    row = lambda i: (i, 0)
    fix = lambda i: (0, 0)
    ys_specs = [pl.BlockSpec((tm, d), functools.partial(lambda i, k: (k * nb + i, 0), k=k)) for k in range(TOP_K)]
    return pl.pallas_call(
        _post_kernel,
        grid=(nb,),
        in_specs=[pl.BlockSpec((tm, d), row)] + ys_specs + [
            pl.BlockSpec((tm, 128), row), pl.BlockSpec((tm, kp), row), pl.BlockSpec((kp, d), fix),
            pl.BlockSpec((d, d), fix), pl.BlockSpec((1, d), fix), pl.BlockSpec((1, d), fix)],
        out_specs=pl.BlockSpec((tm, d), row),
        out_shape=jax.ShapeDtypeStruct((m, d), F32),
        compiler_params=_cparams(("arbitrary",)),
        name="post",
    )(x1, *([ysel] * TOP_K), gt, p, wp, wg, g.reshape(1, d), beta.reshape(1, d))


def _s5_discretize(lam_re, lam_im, b_re, b_im, log_dt):
    dt = jnp.exp(log_dt)[:, None]
    lr = jnp.minimum(lam_re, -1e-4)
    li = lam_im
    mag = jnp.exp(lr * dt)
    ab_re, ab_im = mag * jnp.cos(li * dt), mag * jnp.sin(li * dt)
    nr = ab_re - 1.0
    den = lr * lr + li * li
    zr = (nr * lr + ab_im * li) / den
    zi = (ab_im * lr - nr * li) / den
    bb_re = zr[..., None] * b_re - zi[..., None] * b_im
    bb_im = zr[..., None] * b_im + zi[..., None] * b_re
    return lr * dt, li * dt, ab_re, ab_im, bb_re, bb_im


def _s5_chunk_operators(lam_re, lam_im, b_re, b_im, c_re, c_im, log_dt):
    L = S5_CHUNK
    G, P, H = S5_GROUPS, S5_STATE, S5_GROUP
    ldr, ldi, _, _, bb_re, bb_im = _s5_discretize(lam_re, lam_im, b_re, b_im, log_dt)
    k = jnp.arange(L + 1, dtype=F32)[:, None, None]
    pk_mag = jnp.exp(k * ldr)
    pk_re, pk_im = pk_mag * jnp.cos(k * ldi), pk_mag * jnp.sin(k * ldi)
    abb_re = pk_re[..., None] * bb_re - pk_im[..., None] * bb_im
    abb_im = pk_re[..., None] * bb_im + pk_im[..., None] * bb_re
    kern = (jnp.einsum('gcp,kgpd->kgcd', c_re, abb_re[:L]) - jnp.einsum('gcp,kgpd->kgcd', c_im, abb_im[:L]))
    t = jnp.arange(L)
    lag = t[None, :] - t[:, None]
    toe = jnp.where((lag >= 0)[:, :, None, None, None], kern[jnp.clip(lag, 0, L - 1)], 0.0)
    m_op = jnp.transpose(toe, (2, 0, 4, 1, 3)).reshape(G, L * H, L * H)
    rev_re = abb_re[L - 1 - t]
    rev_im = abb_im[L - 1 - t]
    bs_op = jnp.concatenate([jnp.transpose(rev_re, (1, 0, 3, 2)).reshape(G, L * H, P),
                             jnp.transpose(rev_im, (1, 0, 3, 2)).reshape(G, L * H, P)], -1)
    ca_re = c_re[None] * pk_re[1:, :, None, :] - c_im[None] * pk_im[1:, :, None, :]
    ca_im = c_re[None] * pk_im[1:, :, None, :] + c_im[None] * pk_re[1:, :, None, :]
    cs_op = jnp.concatenate([jnp.transpose(ca_re, (1, 3, 0, 2)).reshape(G, P, L * H),
                             -jnp.transpose(ca_im, (1, 3, 0, 2)).reshape(G, P, L * H)], 1)
    return m_op, bs_op, cs_op, pk_re[L], pk_im[L]


def _s5_state_kernel(u_ref, bs_ref, s_ref, *, gpb):
    for q in range(gpb):
        s_ref[:, q * 128:(q + 1) * 128] = _dot(u_ref[q].astype(BF16), bs_ref[q])


def _s5_scan_kernel(sr_ref, si_ref, ar_ref, ai_ref, x0r_ref, x0i_ref, er_ref, ei_ref):
    nchunk = sr_ref.shape[0]
    ar, ai = ar_ref[...], ai_ref[...]

    def body(j, carry):
        xr, xi = carry
        x0r_ref[j] = xr
        x0i_ref[j] = xi
        return ar * xr - ai * xi + sr_ref[j], ar * xi + ai * xr + si_ref[j]

    zero = jnp.zeros(sr_ref.shape[1:], F32)
    xr, xi = lax.fori_loop(0, nchunk, body, (zero, zero))
    er_ref[...] = xr
    ei_ref[...] = xi


def _s5_out_kernel(u_ref, x0_ref, m_ref, cs_ref, d_ref, o_ref, *, gpb):
    for q in range(gpb):
        u = u_ref[q]
        y = (_dot(u.astype(BF16), m_ref[q]) + _dot(x0_ref[:, q * 128:(q + 1) * 128].astype(BF16), cs_ref[q])
             + d_ref[q] * u)
        o_ref[q] = _gelu(y)


def _s5_prompt(u, ops, d_skip):
    m_op, bs_op, cs_op, al_re, al_im = ops
    bsz, t, _ = u.shape
    L, G, P, H = S5_CHUNK, S5_GROUPS, S5_STATE, S5_GROUP
    nchunk = t // L
    rows = nchunk * bsz
    gpb = 8
    ug = jnp.transpose(u.reshape(bsz, nchunk, L, G, H), (3, 1, 0, 2, 4)).reshape(G, rows, L * H)
    s_loc = pl.pallas_call(
        functools.partial(_s5_state_kernel, gpb=gpb),
        grid=(G // gpb,),
        in_specs=[pl.BlockSpec((gpb, rows, L * H), lambda g: (g, 0, 0)),
                  pl.BlockSpec((gpb, L * H, 2 * P), lambda g: (g, 0, 0))],
        out_specs=pl.BlockSpec((rows, gpb * 2 * P), lambda g: (0, g)),
        out_shape=jax.ShapeDtypeStruct((rows, G * 2 * P), F32),
        compiler_params=_cparams(("arbitrary",)),
        name="s5_state",
    )(ug, bs_op.astype(BF16))
    s4 = s_loc.reshape(nchunk, bsz, G, 2, P)
    sr = s4[:, :, :, 0].reshape(nchunk, bsz, G * P)
    si = s4[:, :, :, 1].reshape(nchunk, bsz, G * P)
    lw = 512
    seq = pl.BlockSpec((nchunk, bsz, lw), lambda i: (0, 0, i))
    vec = pl.BlockSpec((1, lw), lambda i: (0, i))
    fin = pl.BlockSpec((bsz, lw), lambda i: (0, i))
    x0r, x0i, er, ei = pl.pallas_call(
        _s5_scan_kernel,
        grid=(G * P // lw,),
        in_specs=[seq, seq, vec, vec],
        out_specs=[seq, seq, fin, fin],
        out_shape=[jax.ShapeDtypeStruct((nchunk, bsz, G * P), F32)] * 2
        + [jax.ShapeDtypeStruct((bsz, G * P), F32)] * 2,
        compiler_params=_cparams(("arbitrary",)),
        name="s5_scan",
    )(sr, si, al_re.reshape(1, G * P), al_im.reshape(1, G * P))
    x0 = jnp.stack([x0r.reshape(rows, G, P), x0i.reshape(rows, G, P)], 2).reshape(rows, G * 2 * P)
    d_t = jnp.tile(d_skip, (1, L)).reshape(G, 1, L * H)
    g5 = pl.pallas_call(
        functools.partial(_s5_out_kernel, gpb=gpb),
        grid=(G // gpb,),
        in_specs=[pl.BlockSpec((gpb, rows, L * H), lambda g: (g, 0, 0)),
                  pl.BlockSpec((rows, gpb * 2 * P), lambda g: (0, g)),
                  pl.BlockSpec((gpb, L * H, L * H), lambda g: (g, 0, 0)),
                  pl.BlockSpec((gpb, 2 * P, L * H), lambda g: (g, 0, 0)),
                  pl.BlockSpec((gpb, 1, L * H), lambda g: (g, 0, 0))],
        out_specs=pl.BlockSpec((gpb, rows, L * H), lambda g: (g, 0, 0)),
        out_shape=jax.ShapeDtypeStruct((G, rows, L * H), F32),
        compiler_params=_cparams(("arbitrary",)),
        name="s5_out",
    )(ug, x0, m_op.astype(BF16), cs_op.astype(BF16), d_t)
    g5 = jnp.transpose(g5.reshape(G, nchunk, bsz, L, H), (2, 1, 3, 0, 4)).reshape(bsz, t, G * H)
    return g5, er.reshape(bsz, G, P), ei.reshape(bsz, G, P)


def _s5_step_kernel(u_ref, x0r_ref, x0i_ref, ar_ref, ai_ref, bbr_ref, bbi_ref, cr_ref, ci_ref, d_ref,
                    g_ref, xr_ref, xi_ref, *, nblk, bw, sw):
    for q in range(nblk):
        u = u_ref[:, q * bw:(q + 1) * bw]
        ub = u.astype(BF16)
        sl = slice(q * sw, (q + 1) * sw)
        ar, ai = ar_ref[:, sl], ai_ref[:, sl]
        x0r, x0i = x0r_ref[:, sl], x0i_ref[:, sl]
        xr = ar * x0r - ai * x0i + _dot(ub, bbr_ref[q])
        xi = ar * x0i + ai * x0r + _dot(ub, bbi_ref[q])
        xr_ref[:, sl] = xr
        xi_ref[:, sl] = xi
        y = _dot(xr.astype(BF16), cr_ref[q]) - _dot(xi.astype(BF16), ci_ref[q]) + d_ref[:, q * bw:(q + 1) * bw] * u
        g_ref[:, q * bw:(q + 1) * bw] = _gelu(y)


def _s5_step(u, x0r, x0i, lam_re, lam_im, b_re, b_im, c_re, c_im, log_dt, d_skip):
    n = u.shape[0]
    G, P, H = S5_GROUPS, S5_STATE, S5_GROUP
    gb = 16
    nblk = G // gb
    _, _, ab_re, ab_im, bb_re, bb_im = _s5_discretize(lam_re, lam_im, b_re, b_im, log_dt)
    eye = jnp.eye(gb, dtype=F32)

    def in_blocks(w):
        w4 = w.reshape(nblk, gb, P, H)
        return jnp.einsum('qgph,gk->qghkp', w4, eye).reshape(nblk, gb * H, gb * P)

    def out_blocks(w):
        w4 = w.reshape(nblk, gb, H, P)
        return jnp.einsum('qghp,gk->qgpkh', w4, eye).reshape(nblk, gb * P, gb * H)

    full = lambda i: (0, 0)
    full3 = lambda i: (0, 0, 0)
    g5, xr, xi = pl.pallas_call(
        functools.partial(_s5_step_kernel, nblk=nblk, bw=gb * H, sw=gb * P),
        grid=(1,),
        in_specs=[pl.BlockSpec((n, G * H), full), pl.BlockSpec((n, G * P), full), pl.BlockSpec((n, G * P), full),
                  pl.BlockSpec((1, G * P), full), pl.BlockSpec((1, G * P), full),
                  pl.BlockSpec((nblk, gb * H, gb * P), full3), pl.BlockSpec((nblk, gb * H, gb * P), full3),
                  pl.BlockSpec((nblk, gb * P, gb * H), full3), pl.BlockSpec((nblk, gb * P, gb * H), full3),
                  pl.BlockSpec((1, G * H), full)],
        out_specs=[pl.BlockSpec((n, G * H), full), pl.BlockSpec((n, G * P), full), pl.BlockSpec((n, G * P), full)],
        out_shape=[jax.ShapeDtypeStruct((n, G * H), F32), jax.ShapeDtypeStruct((n, G * P), F32),
                   jax.ShapeDtypeStruct((n, G * P), F32)],
        compiler_params=_cparams(("arbitrary",)),
        name="s5_step",
    )(u, x0r.reshape(n, G * P), x0i.reshape(n, G * P), ab_re.reshape(1, G * P), ab_im.reshape(1, G * P),
      in_blocks(bb_re).astype(BF16), in_blocks(bb_im).astype(BF16),
      out_blocks(c_re).astype(BF16), out_blocks(c_im).astype(BF16), d_skip.reshape(1, G * H))
    return g5, xr.reshape(n, G, P), xi.reshape(n, G, P)


def _sg_kernel(u_ref, v_ref, g_ref, b_ref, w_ref, bias_ref, o_ref, vl_ref, *, cpb):
    i = pl.program_id(0)
    v = _ln_rows(v_ref[...], g_ref[...], b_ref[...])
    vb = v.astype(BF16)
    hd = SG_WIDTH // SG_HEADS
    for h in range(SG_HEADS):
        sl = slice(h * hd, (h + 1) * hd)
        s = _dot(w_ref[h], vb[:, sl]) + bias_ref[:, sl]
        o_ref[:, sl] = u_ref[:, sl] * s

    @pl.when(i % cpb == cpb - 1)
    def _():
        vl_ref[...] = v


def _spatial_gating_prompt(z, col0, bsz, t, ln_g, ln_b, w_s, b_s):
    cpb = t // SG_CHUNK
    cu = col0 // SG_WIDTH
    w = (w_s * jnp.tril(jnp.ones((SG_CHUNK, SG_CHUNK), F32))).astype(BF16)
    bias = jnp.repeat(b_s.T, SG_WIDTH // SG_HEADS, axis=1)
    fix = lambda i: (0, 0)
    return pl.pallas_call(
        functools.partial(_sg_kernel, cpb=cpb),
        grid=(bsz * cpb,),
        in_specs=[pl.BlockSpec((SG_CHUNK, SG_WIDTH), lambda i: (i, cu)),
                  pl.BlockSpec((SG_CHUNK, SG_WIDTH), lambda i: (i, cu + 1)),
                  pl.BlockSpec((1, SG_WIDTH), fix), pl.BlockSpec((1, SG_WIDTH), fix),
                  pl.BlockSpec((SG_HEADS, SG_CHUNK, SG_CHUNK), lambda i: (0, 0, 0)),
                  pl.BlockSpec((SG_CHUNK, SG_WIDTH), fix)],
        out_specs=[pl.BlockSpec((SG_CHUNK, SG_WIDTH), lambda i: (i, 0)),
                   pl.BlockSpec((None, SG_CHUNK, SG_WIDTH), lambda i: (i // cpb, 0, 0))],
        out_shape=[jax.ShapeDtypeStruct((bsz * t, SG_WIDTH), F32),
                   jax.ShapeDtypeStruct((bsz, SG_CHUNK, SG_WIDTH), F32)],
        compiler_params=_cparams(("arbitrary",)),
        name="spatial_gating",
    )(z, z, ln_g.reshape(1, -1), ln_b.reshape(1, -1), w, bias)


def _sg_step_kernel(u_ref, v_ref, g_ref, b_ref, w0_ref, b0_ref, o_ref, vn_ref):
    v = _ln_rows(v_ref[...], g_ref[...], b_ref[...])
    vn_ref[...] = v
    o_ref[...] = u_ref[...] * (w0_ref[...] * v + b0_ref[...])


def _spatial_gating_step(zs, ln_g, ln_b, w_s, b_s):
    n = zs.shape[0]
    hd = SG_WIDTH // SG_HEADS
    w0 = jnp.repeat(w_s[:, 0, 0], hd).reshape(1, SG_WIDTH)
    b0 = jnp.repeat(b_s[:, 0], hd).reshape(1, SG_WIDTH)
    fix = lambda i: (0, 0)
    return pl.pallas_call(
        _sg_step_kernel,
        grid=(1,),
        in_specs=[pl.BlockSpec((n, SG_WIDTH), lambda i: (0, 0)), pl.BlockSpec((n, SG_WIDTH), lambda i: (0, 1)),
                  pl.BlockSpec((1, SG_WIDTH), fix), pl.BlockSpec((1, SG_WIDTH), fix),
                  pl.BlockSpec((1, SG_WIDTH), fix), pl.BlockSpec((1, SG_WIDTH), fix)],
        out_specs=[pl.BlockSpec((n, SG_WIDTH), fix), pl.BlockSpec((n, SG_WIDTH), fix)],
        out_shape=[jax.ShapeDtypeStruct((n, SG_WIDTH), F32)] * 2,
        compiler_params=_cparams(("arbitrary",)),
        name="spatial_gating_step",
    )(zs, zs, ln_g.reshape(1, -1), ln_b.reshape(1, -1), w0, b0)


def _online_softmax_update(s, m_sc, l_sc):
    m_old = m_sc[...]
    m_new = jnp.maximum(m_old, jnp.max(s, -1, keepdims=True))
    a = jnp.exp(m_old - m_new)
    p = jnp.exp(s - m_new)
    l_sc[...] = a * l_sc[...] + jnp.sum(p, -1, keepdims=True)
    m_sc[...] = m_new
    return a, p


def _causal_mask(s, qi, kb, tk):
    row = lax.broadcasted_iota(jnp.int32, s.shape, 0)
    col = lax.broadcasted_iota(jnp.int32, s.shape, 1)
    qpos = qi * Q_BLOCK + (row & (Q_BLOCK - 1))
    return jnp.where(qpos >= kb * tk + col, s, NEG_INF)


def _key_block_cases(qi, kb, tk, step):
    needed = kb * tk <= qi * Q_BLOCK + Q_BLOCK - 1
    crosses = (kb + 1) * tk - 1 > qi * Q_BLOCK

    @pl.when(needed & crosses)
    def _():
        step(True)

    @pl.when(needed & jnp.logical_not(crosses))
    def _():
        step(False)


def _mla_prefill_kernel(ql_ref, qp_ref, ckv_ref, kpe_ref, wuv_ref, y_ref, m_sc, l_sc, acc_sc, *, tk):
    qi, kb = pl.program_id(1), pl.program_id(2)

    @pl.when(kb == 0)
    def _():
        m_sc[...] = jnp.full(m_sc.shape, NEG_INF, F32)
        l_sc[...] = jnp.zeros(l_sc.shape, F32)
        acc_sc[...] = jnp.zeros(acc_sc.shape, F32)

    def step(masked):
        k1 = ckv_ref[...].astype(BF16)
        s = _dot_nt(ql_ref[...], k1) + _dot_nt(qp_ref[...], kpe_ref[...].astype(BF16))
        if masked:
            s = _causal_mask(s, qi, kb, tk)
        a, p = _online_softmax_update(s, m_sc, l_sc)
        acc_sc[...] = a * acc_sc[...] + _dot(p.astype(BF16), k1)

    _key_block_cases(qi, kb, tk, step)

    @pl.when(kb == pl.num_programs(2) - 1)
    def _():
        o = (acc_sc[...] / l_sc[...]).astype(BF16)
        for h in range(MLA_HEADS):
            y_ref[:, h * MLA_V:(h + 1) * MLA_V] = _dot(o[h * Q_BLOCK:(h + 1) * Q_BLOCK],
                                                        wuv_ref[:, h * MLA_V:(h + 1) * MLA_V])


def _mla_prefill(q_lat, q_pe, c_kv, k_pe, w_uv, *, tk=512):
    bsz, t, nh, r = q_lat.shape
    nq = t // Q_BLOCK
    tk = min(tk, t)
    rows = nh * Q_BLOCK

    scale = (MLA_NOPE + MLA_ROPE) ** -0.5

    def blocks(q):
        d = q.shape[-1]
        q = jnp.transpose((q * scale).reshape(bsz, nq, Q_BLOCK, nh, d), (0, 1, 3, 2, 4))
        return q.reshape(bsz, nq, rows, d).astype(BF16)

    kmap = lambda b, qi, kb: (b, jnp.minimum(kb, (qi * Q_BLOCK + Q_BLOCK - 1) // tk), 0)
    return pl.pallas_call(
        functools.partial(_mla_prefill_kernel, tk=tk),
        grid=(bsz, nq, t // tk),
        in_specs=[pl.BlockSpec((None, None, rows, r), lambda b, qi, kb: (b, qi, 0, 0)),
                  pl.BlockSpec((None, None, rows, MLA_ROPE), lambda b, qi, kb: (b, qi, 0, 0)),
                  pl.BlockSpec((None, tk, r), kmap),
                  pl.BlockSpec((None, tk, MLA_ROPE), kmap),
                  pl.BlockSpec((r, nh * MLA_V), lambda b, qi, kb: (0, 0))],
        out_specs=pl.BlockSpec((None, Q_BLOCK, nh * MLA_V), lambda b, qi, kb: (b, qi, 0)),
        out_shape=jax.ShapeDtypeStruct((bsz, t, nh * MLA_V), F32),
        scratch_shapes=[pltpu.VMEM((rows, 1), F32), pltpu.VMEM((rows, 1), F32), pltpu.VMEM((rows, r), F32)],
        compiler_params=_cparams(("arbitrary", "arbitrary", "arbitrary")),
        name="mla_prefill",
    )(blocks(q_lat), blocks(q_pe), c_kv, k_pe, w_uv.reshape(r, nh * MLA_V).astype(BF16))


def _diff_finalize(o0, o1, lam, subln, lam_init):
    o = o0 - lam * o1
    ms = jnp.mean(o * o, -1, keepdims=True)
    return o * lax.rsqrt(ms + 1e-5) * subln * (1.0 - lam_init)


def _diff_prefill_kernel(q_ref, k_ref, v_ref, lam_ref, sub_ref, y_ref, m_sc, l_sc, acc_sc, *, tk, lam_init):
    qi, kb = pl.program_id(1), pl.program_id(2)

    @pl.when(kb == 0)
    def _():
        m_sc[...] = jnp.full(m_sc.shape, NEG_INF, F32)
        l_sc[...] = jnp.zeros(l_sc.shape, F32)
        acc_sc[...] = jnp.zeros(acc_sc.shape, F32)

    def step(masked):
        vb = v_ref[...].astype(BF16)
        for mi in range(2):
            s = _dot_nt(q_ref[mi], k_ref[mi].astype(BF16))
            if masked:
                s = _causal_mask(s, qi, kb, tk)
            a, p = _online_softmax_update(s, m_sc.at[mi], l_sc.at[mi])
            acc_sc[mi] = a * acc_sc[mi] + _dot(p.astype(BF16), vb)

    _key_block_cases(qi, kb, tk, step)

    @pl.when(kb == pl.num_programs(2) - 1)
    def _():
        o = _diff_finalize(acc_sc[0] / l_sc[0], acc_sc[1] / l_sc[1], lam_ref[...], sub_ref[...], lam_init)
        for g in range(DIFF_GROUP):
            y_ref[:, g * DIFF_V_DIM:(g + 1) * DIFF_V_DIM] = o[g * Q_BLOCK:(g + 1) * Q_BLOCK]


def _diff_prefill(dq, dk, dv, lam, subln, lam_init, *, tk=512):
    bsz, t = dq.shape[:2]
    nq = t // Q_BLOCK
    tk = min(tk, t)
    rows = DIFF_GROUP * Q_BLOCK
    d = DIFF_HEAD_DIM
    q = (dq * d ** -0.5).reshape(bsz, nq, Q_BLOCK, DIFF_KV_HEADS, DIFF_GROUP, 2, d)
    q = jnp.transpose(q, (0, 3, 1, 5, 4, 2, 6))
    q = q.reshape(bsz * DIFF_KV_HEADS, nq, 2, rows, d).astype(BF16)
    k = jnp.transpose(dk, (0, 2, 3, 1, 4)).reshape(bsz * DIFF_KV_HEADS, 2, t, d)
    v = jnp.transpose(dv, (0, 2, 1, 3)).reshape(bsz * DIFF_KV_HEADS, t, DIFF_V_DIM)
    last = lambda qi: (qi * Q_BLOCK + Q_BLOCK - 1) // tk
    width = DIFF_GROUP * DIFF_V_DIM
    return pl.pallas_call(
        functools.partial(_diff_prefill_kernel, tk=tk, lam_init=lam_init),
        grid=(bsz * DIFF_KV_HEADS, nq, t // tk),
        in_specs=[pl.BlockSpec((None, None, 2, rows, d), lambda a, qi, kb: (a, qi, 0, 0, 0)),
                  pl.BlockSpec((None, 2, tk, d), lambda a, qi, kb: (a, 0, jnp.minimum(kb, last(qi)), 0)),
                  pl.BlockSpec((None, tk, DIFF_V_DIM), lambda a, qi, kb: (a, jnp.minimum(kb, last(qi)), 0)),
                  pl.BlockSpec((1, DIFF_V_DIM), lambda a, qi, kb: (0, 0)),
                  pl.BlockSpec((1, DIFF_V_DIM), lambda a, qi, kb: (0, 0))],
        out_specs=pl.BlockSpec((None, Q_BLOCK, width),
                               lambda a, qi, kb: (a // DIFF_KV_HEADS, qi, a % DIFF_KV_HEADS)),
        out_shape=jax.ShapeDtypeStruct((bsz, t, DIFF_KV_HEADS * width), F32),
        scratch_shapes=[pltpu.VMEM((2, rows, 1), F32), pltpu.VMEM((2, rows, 1), F32),
                        pltpu.VMEM((2, rows, DIFF_V_DIM), F32)],
        compiler_params=_cparams(("arbitrary", "arbitrary", "arbitrary")),
        name="diff_prefill",
    )(q, k, v, jnp.full((1, DIFF_V_DIM), lam, F32), subln.reshape(1, DIFF_V_DIM))


def _decode_init(s0_ref, v0_ref, m_sc, l_sc, acc_sc):
    m_sc[...] = s0_ref[...]
    l_sc[...] = jnp.ones(l_sc.shape, F32)
    acc_sc[...] = jnp.broadcast_to(v0_ref[...], acc_sc.shape)


def _mla_decode_kernel(pt_ref, *refs, npg, scale):
    del pt_ref
    q1_ref, q2_ref, s0_ref, v0_ref = refs[:4]
    lat_refs = refs[4:4 + npg]
    pet_refs = refs[4 + npg:4 + 2 * npg]
    o_ref, m_sc, l_sc, acc_sc = refs[4 + 2 * npg:]
    step = pl.program_id(1)

    @pl.when(step == 0)
    def _():
        _decode_init(s0_ref, v0_ref, m_sc, l_sc, acc_sc)

    q1, q2 = q1_ref[...], q2_ref[...]
    pages, scores = [], []
    for i in range(npg):
        lat = lat_refs[i][...].astype(BF16)
        scores.append(_dot_nt(q1, lat) + _dot(q2, pet_refs[i][...].astype(BF16)))
        pages.append(lat)
    s = jnp.concatenate(scores, axis=1) * scale
    a, p = _online_softmax_update(s, m_sc, l_sc)
    pb = p.astype(BF16)
    pw = scores[0].shape[1]
    acc = a * acc_sc[...]
    for i in range(npg):
        acc = acc + _dot(pb[:, i * pw:(i + 1) * pw], pages[i])
    acc_sc[...] = acc

    @pl.when(step == pl.num_programs(1) - 1)
    def _():
        o_ref[...] = acc_sc[...] / l_sc[...]


def _diff_decode_kernel(pt_ref, *refs, npg, page, scale):
    del pt_ref
    q_ref, s0_ref, v0_ref = refs[:3]
    kt_refs = refs[3:3 + npg]
    v_refs = refs[3 + npg:3 + 2 * npg]
    o_ref, m_sc, l_sc, acc_sc = refs[3 + 2 * npg:]
    step = pl.program_id(1)

    @pl.when(step == 0)
    def _():
        _decode_init(s0_ref, v0_ref, m_sc, l_sc, acc_sc)

    q = q_ref[...]
    scores = [_dot(q, kt_refs[i][...].astype(BF16)) for i in range(npg)]
    s = jnp.concatenate(scores, axis=1) * scale
    a, p = _online_softmax_update(s, m_sc, l_sc)
    pb = p.astype(BF16)
    for h in range(DIFF_KV_HEADS):
        sl = slice(h * DIFF_V_DIM, (h + 1) * DIFF_V_DIM)
        acc = a * acc_sc[:, sl]
        for i in range(npg):
            vh = v_refs[i][pl.ds(h, page, stride=DIFF_KV_HEADS), :].astype(BF16)
            acc = acc + _dot(pb[:, i * page:(i + 1) * page], vh)
        acc_sc[:, sl] = acc

    @pl.when(step == pl.num_programs(1) - 1)
    def _():
        o_ref[...] = acc_sc[...] / l_sc[...]


def _paged_call(kernel_fn, name, page_table, small, pools, r, dv):
    bsz, npages = page_table.shape
    npg = math.gcd(PAGES_PER_STEP, npages)
    per_b = lambda b, s, pt: (b, 0, 0)
    in_specs = [pl.BlockSpec((None,) + a.shape[1:], per_b) for a in small]
    args = list(small)
    for pool in pools:
        in_specs += [pl.BlockSpec((None,) + pool.shape[1:],
                                  functools.partial(lambda b, s, pt, i: (pt[b, s * npg + i], 0, 0), i=i))
                     for i in range(npg)]
        args += [pool] * npg
    grid_spec = pltpu.PrefetchScalarGridSpec(
        num_scalar_prefetch=1,
        grid=(bsz, npages // npg),
        in_specs=in_specs,
        out_specs=pl.BlockSpec((None, r, dv), per_b),
        scratch_shapes=[pltpu.VMEM((r, 1), F32), pltpu.VMEM((r, 1), F32), pltpu.VMEM((r, dv), F32)],
    )
    return pl.pallas_call(
        functools.partial(kernel_fn, npg=npg),
        grid_spec=grid_spec,
        out_shape=jax.ShapeDtypeStruct((bsz, r, dv), F32),
        compiler_params=_cparams(("arbitrary", "arbitrary")),
        name=name,
    )(page_table, *args)


def _moe_kernel(first_ref, e_ref, nsub_ref, oblk_ref, *refs):
    del first_ref, e_ref, oblk_ref
    x_refs = refs[:MOE_NSUB]
    wg_ref, bg_ref, wu_ref, bu_ref, wd_ref, bd_ref, o_ref, wg_sc, wu_sc, wd_sc = refs[MOE_NSUB:]
    s, j = pl.program_id(0), pl.program_id(1)
    ns = nsub_ref[s]

    @pl.when(ns > 0)
    def _():
        @pl.when(j == 0)
        def _():
            o_ref[...] = jnp.broadcast_to(bd_ref[...], o_ref.shape)

        for cnt in range(1, MOE_NSUB + 1):
            @pl.when(ns == cnt)
            def _(cnt=cnt):
                wg_sc[...] = wg_ref[...].astype(BF16)
                wu_sc[...] = wu_ref[...].astype(BF16)
                wd_sc[...] = wd_ref[...].astype(BF16)
                for i in range(cnt):
                    xb = x_refs[i][...].astype(BF16)
                    g = jnp.minimum(_dot(xb, wg_sc[...]) + bg_ref[...], SWIGLU_LIMIT)
                    u = jnp.clip(_dot(xb, wu_sc[...]) + bu_ref[...], -SWIGLU_LIMIT, SWIGLU_LIMIT)
                    h = (u + 1.0) * (g * _sigmoid(SWIGLU_ALPHA * g))
                    o_ref[i * MOE_SUB:(i + 1) * MOE_SUB, :] += _dot(h.astype(BF16), wd_sc[...])


def _moe(x, logits, layer, wg, bg, wu, bu, wd, bd):
    n, d = x.shape
    f = wg.shape[-1]
    sub, nsub_max, tf = MOE_SUB, MOE_NSUB, MOE_TF
    tm = sub * nsub_max
    nf = f // tf
    nk = n * TOP_K
    nblk = nk // sub + N_EXPERTS
    nsb = (nblk + (nsub_max - 1) * N_EXPERTS) // nsub_max
    top_v, top_i = lax.top_k(logits, TOP_K)
    gates = jax.nn.softmax(top_v, axis=-1)
    flat_e = top_i.reshape(-1).astype(I32)
    order = jnp.argsort(flat_e).astype(I32)
    inv = jnp.argsort(order).astype(I32)
    eids = jnp.arange(N_EXPERTS, dtype=I32)
    counts = jnp.sum((flat_e[:, None] == eids[None, :]).astype(I32), axis=0)
    grp_start = (jnp.cumsum(counts) - counts).astype(I32)
    blocks_e = (counts + sub - 1) // sub
    blk_end = jnp.cumsum(blocks_e).astype(I32)
    blk_start = blk_end - blocks_e
    sb_per_e = (blocks_e + nsub_max - 1) // nsub_max
    sb_end = jnp.cumsum(sb_per_e).astype(I32)
    sb_start = sb_end - sb_per_e
    r = jnp.arange(nblk * sub, dtype=I32)
    e_r = jnp.minimum(jnp.sum((r[:, None] // sub >= blk_end[None, :]).astype(I32), axis=1), N_EXPERTS - 1)
    rank_r = r - blk_start[e_r] * sub
    valid_r = (r // sub < blk_end[-1]) & (rank_r < counts[e_r])
    tok = jnp.where(valid_r, order[jnp.clip(grp_start[e_r] + rank_r, 0, nk - 1)] // TOP_K, 0)
    xs = x[tok]
    n_used = sb_end[-1]
    sb_ids = jnp.arange(nsb, dtype=I32)
    sbc = jnp.minimum(sb_ids, n_used - 1)
    sb_e = jnp.minimum(jnp.sum((sbc[:, None] >= sb_end[None, :]).astype(I32), axis=1), N_EXPERTS - 1)
    kk = sbc - sb_start[sb_e]
    first = blk_start[sb_e] + kk * nsub_max
    nsub = jnp.where(sb_ids < n_used, jnp.clip(blocks_e[sb_e] - kk * nsub_max, 0, nsub_max), 0).astype(I32)

    jj = lambda s, j, ns: jnp.where(ns[s] > 0, j, nf - 1)

    def x_spec(i):
        return pl.BlockSpec((sub, d), lambda s, j, fr, e, ns, ob: (jnp.minimum(fr[s] + i, nblk - 1), 0))

    grid_spec = pltpu.PrefetchScalarGridSpec(
        num_scalar_prefetch=4,
        grid=(nsb, nf),
        in_specs=[x_spec(i) for i in range(nsub_max)] + [
            pl.BlockSpec((None, None, d, tf), lambda s, j, fr, e, ns, ob: (layer, e[s], 0, jj(s, j, ns))),
            pl.BlockSpec((None, None, 1, tf), lambda s, j, fr, e, ns, ob: (layer, e[s], 0, jj(s, j, ns))),
            pl.BlockSpec((None, None, d, tf), lambda s, j, fr, e, ns, ob: (layer, e[s], 0, jj(s, j, ns))),
            pl.BlockSpec((None, None, 1, tf), lambda s, j, fr, e, ns, ob: (layer, e[s], 0, jj(s, j, ns))),
            pl.BlockSpec((None, None, tf, d), lambda s, j, fr, e, ns, ob: (layer, e[s], jj(s, j, ns), 0)),
            pl.BlockSpec((None, None, 1, d), lambda s, j, fr, e, ns, ob: (layer, e[s], 0, 0))],
        out_specs=pl.BlockSpec((tm, d), lambda s, j, fr, e, ns, ob: (ob[s], 0)),
        scratch_shapes=[pltpu.VMEM((d, tf), BF16), pltpu.VMEM((d, tf), BF16), pltpu.VMEM((tf, d), BF16)],
    )
    nl, ne = wg.shape[:2]
    yb = pl.pallas_call(
        _moe_kernel,
        grid_spec=grid_spec,
        out_shape=jax.ShapeDtypeStruct((nsb * tm, d), F32),
        compiler_params=_cparams(("arbitrary", "arbitrary")),
        name="moe_experts",
    )(first.astype(I32), sb_e.astype(I32), nsub, sbc.astype(I32), *([xs] * nsub_max),
      wg, bg.reshape(nl, ne, 1, f), wu, bu.reshape(nl, ne, 1, f), wd, bd.reshape(nl, ne, 1, d))
    rank_a = inv - grp_start[flat_e]
    dest = (sb_start[flat_e] + rank_a // tm) * tm + rank_a % tm
    return yb[dest.reshape(n, TOP_K).T.reshape(-1)], gates


def _rmsnorm(x, g, eps=1e-6):
    return x * lax.rsqrt(jnp.mean(x * x, -1, keepdims=True) + eps) * g


def _rope(x, pos):
    d = x.shape[-1]
    inv = ROPE_THETA ** (-jnp.arange(0, d, 2, dtype=F32) / d)
    ang = pos.astype(F32)[:, None] * inv[None, :]
    ang = ang.reshape((ang.shape[0],) + (1,) * (x.ndim - 2) + (ang.shape[1],))
    cos, sin = jnp.cos(ang), jnp.sin(ang)
    x1, x2 = jnp.split(x, 2, axis=-1)
    return jnp.concatenate([x1 * cos - x2 * sin, x1 * sin + x2 * cos], -1)


def _layer_ab(x, np_rows, bsz, t, state_re, state_im, w):
    (w_in, lam_re, lam_im, b_re, b_im, c_re, c_im, d_skip, log_dt, w_glu, b_glu,
     ln_g, ln_b, w_s, b_s) = w
    z = _mm(x, w_in.astype(BF16), tm=640, tn=512, gelu_from=S5_WIDTH // 512)
    ops = _s5_chunk_operators(lam_re, lam_im, b_re, b_im, c_re, c_im, log_dt)
    g5_p, pr, pi = _s5_prompt(z[:np_rows, :S5_WIDTH].reshape(bsz, t, S5_WIDTH), ops, d_skip)
    g5_s, sr, si = _s5_step(z[np_rows:, :S5_WIDTH], state_re, state_im,
                            lam_re, lam_im, b_re, b_im, c_re, c_im, log_dt, d_skip)
    g5 = jnp.concatenate([g5_p.reshape(np_rows, S5_WIDTH), g5_s], 0)
    o5 = _glu(g5, w_glu.astype(BF16), b_glu.reshape(1, -1), tm=640, tn=512)
    osg_p, v_p = _spatial_gating_prompt(z, S5_WIDTH, bsz, t, ln_g, ln_b, w_s, b_s)
    osg_s, v_s = _spatial_gating_step(z[np_rows:, S5_WIDTH:], ln_g, ln_b, w_s, b_s)
    osg = jnp.concatenate([osg_p, osg_s], 0)
    return o5, osg, (pr, pi, sr, si, v_p, v_s[:, None, :])


def _layer_cd(x, np_rows, bsz, t, pos, layer, caches, page_table, w):
    (w_in, q_norm, kv_norm, w_uq, w_uk, w_uv, lq1, lk1, lq2, lk2, subln) = w
    lat_pool, pet_pool, dkt_pool, dv_pool = caches
    n = x.shape[0]
    ns = n - np_rows
    nh = MLA_HEADS
    page = lat_pool.shape[1]
    a1, a2 = MLA_Q_RANK + MLA_KV_RANK, MLA_Q_RANK + MLA_KV_RANK + MLA_ROPE
    ndq = DIFF_HEADS * 2 * DIFF_HEAD_DIM
    ndk = DIFF_KV_HEADS * 2 * DIFF_HEAD_DIM
    ndv = DIFF_KV_HEADS * DIFF_V_DIM
    w_perm = jnp.concatenate([w_in[:, :a1], w_in[:, a2:], w_in[:, a1:a2],
                              jnp.zeros((w_in.shape[0], 128 - MLA_ROPE), F32)], 1).astype(BF16)
    z = _mm(x, w_perm, tm=640, tn=w_perm.shape[1])
    c = 0
    q_lat = z[:, c:c + MLA_Q_RANK]; c += MLA_Q_RANK
    kv_lat = z[:, c:c + MLA_KV_RANK]; c += MLA_KV_RANK
    dq = z[:, c:c + ndq]; c += ndq
    dk = z[:, c:c + ndk]; c += ndk
    dv = z[:, c:c + ndv]; c += ndv
    k_pe = z[:, c:c + MLA_ROPE]
    q = _mm(_rmsnorm(q_lat, q_norm), w_uq.reshape(MLA_Q_RANK, -1).astype(BF16), tm=640, tn=512)
    q = q.reshape(n, nh, MLA_NOPE + MLA_ROPE)
    q_pe = _rope(q[..., MLA_NOPE:], pos)
    q_abs = _bmm(jnp.transpose(q[..., :MLA_NOPE], (1, 0, 2)),
                 jnp.transpose(w_uk, (1, 2, 0)).astype(BF16), tm=640)
    q_abs = jnp.transpose(q_abs, (1, 0, 2))
    c_kv = _rmsnorm(kv_lat, kv_norm)
    k_pe = _rope(k_pe, pos)
    dq = _rope(dq.reshape(n, DIFF_KV_HEADS, DIFF_GROUP, 2, DIFF_HEAD_DIM), pos)
    dk = _rope(dk.reshape(n, DIFF_KV_HEADS, 2, DIFF_HEAD_DIM), pos)
    dv = dv.reshape(n, DIFF_KV_HEADS, DIFF_V_DIM)
    lam_init = 0.8 - 0.6 * math.exp(-0.3 * layer)
    lam = jnp.exp(jnp.sum(lq1 * lk1)) - jnp.exp(jnp.sum(lq2 * lk2)) + lam_init

    pr = lambda a: a[:np_rows].reshape((bsz, t) + a.shape[1:])
    y_mla_p = _mla_prefill(pr(q_abs), pr(q_pe), pr(c_kv), pr(k_pe), w_uv)
    y_diff_p = _diff_prefill(pr(dq), pr(dk), pr(dv), lam, subln, lam_init)

    sm = lambda a: a[np_rows:]
    mla_scale = (MLA_NOPE + MLA_ROPE) ** -0.5
    s0 = (jnp.einsum('bhr,br->bh', sm(q_abs), sm(c_kv)) + jnp.einsum('bhd,bd->bh', sm(q_pe), sm(k_pe))) * mla_scale
    rpad = DECODE_ROWS - nh
    padr = lambda a: jnp.pad(a, ((0, 0), (0, rpad)) + ((0, 0),) * (a.ndim - 2))
    o_mla = _paged_call(functools.partial(_mla_decode_kernel, scale=mla_scale), "mla_decode", page_table,
                        [padr(sm(q_abs)).astype(BF16), padr(sm(q_pe)).astype(BF16), padr(s0)[..., None],
                         sm(c_kv)[:, None, :]],
                        [lat_pool, pet_pool], DECODE_ROWS, MLA_KV_RANK)[:, :nh]
    y_mla_s = _bmm(jnp.transpose(o_mla, (1, 0, 2)), jnp.transpose(w_uv, (1, 0, 2)).astype(BF16), tm=ns)
    y_mla_s = jnp.transpose(y_mla_s, (1, 0, 2)).reshape(ns, nh * MLA_V)

    diff_scale = DIFF_HEAD_DIM ** -0.5
    dq_s, dk_s, dv_s = sm(dq), sm(dk), sm(dv)
    qrow = jnp.transpose(dq_s, (0, 1, 3, 2, 4))
    sel = jnp.eye(DIFF_KV_HEADS * 2, dtype=F32).reshape(DIFF_KV_HEADS, 2, 1, DIFF_KV_HEADS * 2, 1)
    qblk = (qrow[:, :, :, :, None, :] * sel[None]).reshape(ns, DECODE_ROWS, DIFF_KV_HEADS * 2 * DIFF_HEAD_DIM)
    s0d = jnp.einsum('brk,bk->br', qblk, dk_s.reshape(ns, -1)) * diff_scale
    o_d = _paged_call(functools.partial(_diff_decode_kernel, page=page, scale=diff_scale), "diff_decode",
                      page_table, [qblk.astype(BF16), s0d[..., None], dv_s.reshape(ns, 1, -1)],
                      [dkt_pool, dv_pool], DECODE_ROWS, DIFF_KV_HEADS * DIFF_V_DIM)
    o_d = o_d.reshape(ns, DIFF_KV_HEADS, 2, DIFF_GROUP, DIFF_KV_HEADS, DIFF_V_DIM)
    o_d = jnp.stack([o_d[:, h, :, :, h] for h in range(DIFF_KV_HEADS)], 1)
    y_diff_s = _diff_finalize(o_d[:, :, 0], o_d[:, :, 1], lam, subln, lam_init).reshape(ns, -1)

    y_mla = jnp.concatenate([y_mla_p.reshape(np_rows, -1), y_mla_s], 0)
    y_diff = jnp.concatenate([y_diff_p.reshape(np_rows, -1), y_diff_s], 0)
    outs = (pr(c_kv), pr(k_pe), sm(c_kv)[:, None], sm(k_pe)[:, None],
            pr(dk), pr(dv), sm(dk)[:, None], sm(dv)[:, None])
    return y_mla, y_diff, outs


def kernel(x_prompt, x_sample, state_s5_re, state_s5_im, cache_mla_latent, cache_mla_krope, cache_diff_k, cache_diff_v, page_table, p_prompt, p_sample, ab_w_in, s5_lambda_re, s5_lambda_im, s5_b_re, s5_b_im, s5_c_re, s5_c_im, s5_d, s5_log_dt, s5_w_glu, s5_b_glu, sg_ln_g, sg_ln_b, sg_w_s, sg_b_s, ab_w_out, cd_w_in, mla_q_norm, mla_kv_norm, mla_w_uq, mla_w_uk, mla_w_uv, diff_lq1, diff_lk1, diff_lq2, diff_lk2, diff_subln, cd_w_out, ln1_g, ln1_b, ln2_g, ln2_b, router_w, router_b, ex_w_gate, ex_b_gate, ex_w_up, ex_b_up, ex_w_down, ex_b_down, ple_w_proj, ple_w_gate):
    bsz, t, d = x_prompt.shape
    nb, ts, _ = x_sample.shape
    assert ts == 1
    np_rows = bsz * t
    npool, page = cache_mla_latent.shape[1:3]
    past_len = page_table.shape[1] * page
    pos = jnp.concatenate([jnp.tile(jnp.arange(t, dtype=jnp.int32), bsz),
                           jnp.full((nb,), past_len, jnp.int32)])
    x = jnp.concatenate([x_prompt.reshape(np_rows, d), x_sample.reshape(nb, d)], 0)
    p_all = jnp.concatenate([p_prompt.reshape(DEPTH, np_rows, -1), p_sample.reshape(DEPTH, nb, -1)], 1)
    ab_outs, cd_outs = [], []
    for i in range(DEPTH):
        j = i // 2
        if i % 2 == 0:
            w = (ab_w_in[j], s5_lambda_re[j], s5_lambda_im[j], s5_b_re[j], s5_b_im[j], s5_c_re[j], s5_c_im[j],
                 s5_d[j], s5_log_dt[j], s5_w_glu[j], s5_b_glu[j], sg_ln_g[j], sg_ln_b[j], sg_w_s[j], sg_b_s[j])
            a, b, outs = _layer_ab(x, np_rows, bsz, t, state_s5_re[j], state_s5_im[j], w)
            ab_outs.append(outs)
            w_out = ab_w_out[j]
        else:
            w = (cd_w_in[j], mla_q_norm[j], mla_kv_norm[j], mla_w_uq[j], mla_w_uk[j], mla_w_uv[j],
                 diff_lq1[j], diff_lk1[j], diff_lq2[j], diff_lk2[j], diff_subln[j])
            caches = (cache_mla_latent[j].reshape(npool, page, -1),
                      jnp.swapaxes(cache_mla_krope[j], 1, 2),
                      jnp.transpose(cache_diff_k[j], (0, 2, 3, 4, 1)).reshape(npool, -1, page),
                      cache_diff_v[j].reshape(npool, page * DIFF_KV_HEADS, DIFF_V_DIM))
            a, b, outs = _layer_cd(x, np_rows, bsz, t, pos, i, caches, page_table, w)
            cd_outs.append(outs)
            w_out = cd_w_out[j]
        ka = a.shape[1]
        x1, logits = _mix_ln_router(x, a, b, w_out[:ka].astype(BF16), w_out[ka:].astype(BF16),
                                    ln1_g[i], ln1_b[i], router_w[i], router_b[i], tm=520)
        ysel, gates = _moe(x1, logits, i, ex_w_gate, ex_b_gate, ex_w_up, ex_b_up, ex_w_down, ex_b_down)
        x = _post(x1, ysel, gates, p_all[i], ple_w_proj[i].astype(BF16), ple_w_gate[i].astype(BF16),
                  ln2_g[i], ln2_b[i], tm=208)
    y_prompt = x[:np_rows].reshape(bsz, t, d)
    y_sample = x[np_rows:].reshape(nb, ts, d)
    st = lambda k: jnp.stack([o[k] for o in ab_outs])
    sc = lambda k: jnp.stack([o[k] for o in cd_outs])
    return (y_prompt, y_sample, st(0), st(1), st(2), st(3), st(4), st(5),
            sc(0), sc(1), sc(2), sc(3), sc(4), sc(5), sc(6), sc(7))
```

```python
import functools
import math

import jax
import jax.numpy as jnp
from jax import lax
from jax.experimental import pallas as pl
from jax.experimental.pallas import tpu as pltpu

F32 = jnp.float32
BF16 = jnp.bfloat16
I32 = jnp.int32

D_MODEL = 2048
DEPTH = 2
S5_WIDTH = 1024
S5_GROUP = 16
S5_GROUPS = 64
S5_STATE = 64
S5_CHUNK = 16
SG_HEADS = 8
SG_WIDTH = 1024
SG_CHUNK = 128
MLA_HEADS = 8
MLA_Q_RANK = 512
MLA_KV_RANK = 256
MLA_NOPE = 128
MLA_ROPE = 64
MLA_V = 128
DIFF_HEADS = 8
DIFF_KV_HEADS = 2
DIFF_GROUP = 4
DIFF_HEAD_DIM = 64
DIFF_V_DIM = 128
N_EXPERTS = 32
TOP_K = 4
SWIGLU_LIMIT = 7.0
SWIGLU_ALPHA = 1.702
PLE_DIM = 256
ROPE_THETA = 10000.0
Q_BLOCK = 128
ALPHA = (2 * DEPTH) ** 0.25
NEG_INF = -1e30

MOE_SUB = 256
MOE_NSUB = 4
MOE_TF = 256
PAGES_PER_STEP = 32
DECODE_ROWS = 16
VMEM_LIMIT = 56 * 1024 * 1024


def _cparams(sem):
    return pltpu.CompilerParams(dimension_semantics=sem, vmem_limit_bytes=VMEM_LIMIT)


def _row_tile(rows, target):
    best = 8
    for cand in range(8, min(rows, target) + 1, 8):
        if rows % cand == 0:
            best = cand
    return best


def _gelu(x):
    return 0.5 * x * (1.0 + jnp.tanh(0.7978845608028654 * (x + 0.044715 * (x * x * x))))


def _sigmoid(x):
    return 1.0 / (1.0 + jnp.exp(-x))


def _dot(a, b):
    return jnp.dot(a, b, preferred_element_type=F32)


def _dot_nt(a, b):
    return lax.dot_general(a, b, (((1,), (1,)), ((), ())), preferred_element_type=F32)


def _ln_rows(x, g, b, eps=1e-5):
    mu = jnp.mean(x, -1, keepdims=True)
    xc = x - mu
    var = jnp.mean(xc * xc, -1, keepdims=True)
    return xc * lax.rsqrt(var + eps) * g + b


def _mm_kernel(x_ref, w_ref, o_ref, *, gelu_from):
    acc = _dot(x_ref[...].astype(BF16), w_ref[...])
    if gelu_from is None:
        o_ref[...] = acc
    else:
        j = pl.program_id(0)

        @pl.when(j < gelu_from)
        def _():
            o_ref[...] = acc

        @pl.when(j >= gelu_from)
        def _():
            o_ref[...] = _gelu(acc)


def _mm(x, w, *, tm, tn, gelu_from=None):
    m, k = x.shape
    n = w.shape[1]
    tm = _row_tile(m, tm)
    return pl.pallas_call(
        functools.partial(_mm_kernel, gelu_from=gelu_from),
        grid=(n // tn, m // tm),
        in_specs=[pl.BlockSpec((tm, k), lambda j, i: (i, 0)),
                  pl.BlockSpec((k, tn), lambda j, i: (0, j))],
        out_specs=pl.BlockSpec((tm, tn), lambda j, i: (i, j)),
        out_shape=jax.ShapeDtypeStruct((m, n), F32),
        compiler_params=_cparams(("arbitrary", "arbitrary")),
        name="mm",
    )(x, w)


def _bmm_kernel(x_ref, w_ref, o_ref):
    o_ref[...] = _dot(x_ref[...].astype(BF16), w_ref[...])


def _bmm(x, w, *, tm):
    h, m, k = x.shape
    n = w.shape[2]
    tm = _row_tile(m, tm)
    return pl.pallas_call(
        _bmm_kernel,
        grid=(h, m // tm),
        in_specs=[pl.BlockSpec((None, tm, k), lambda a, i: (a, i, 0)),
                  pl.BlockSpec((None, k, n), lambda a, i: (a, 0, 0))],
        out_specs=pl.BlockSpec((None, tm, n), lambda a, i: (a, i, 0)),
        out_shape=jax.ShapeDtypeStruct((h, m, n), F32),
        compiler_params=_cparams(("arbitrary", "arbitrary")),
        name="bmm",
    )(x, w)


def _glu_kernel(g_ref, gcol_ref, w_ref, b_ref, o_ref):
    acc = _dot(g_ref[...].astype(BF16), w_ref[...]) + b_ref[...]
    o_ref[...] = gcol_ref[...] * _sigmoid(acc)


def _glu(g5, w, b, *, tm, tn):
    m, k = g5.shape
    n = w.shape[1]
    tm = _row_tile(m, tm)
    return pl.pallas_call(
        _glu_kernel,
        grid=(n // tn, m // tm),
        in_specs=[pl.BlockSpec((tm, k), lambda j, i: (i, 0)),
                  pl.BlockSpec((tm, tn), lambda j, i: (i, j)),
                  pl.BlockSpec((k, tn), lambda j, i: (0, j)),
                  pl.BlockSpec((1, tn), lambda j, i: (0, j))],
        out_specs=pl.BlockSpec((tm, tn), lambda j, i: (i, j)),
        out_shape=jax.ShapeDtypeStruct((m, n), F32),
        compiler_params=_cparams(("arbitrary", "arbitrary")),
        name="glu",
    )(g5, g5, w, b)


def _mix_ln_router_kernel(r_ref, a_ref, b_ref, wa_ref, wb_ref, g_ref, beta_ref,
                          wrh_ref, wrl_ref, br_ref, x_ref, lg_ref):
    mix = _dot(a_ref[...].astype(BF16), wa_ref[...]) + _dot(b_ref[...].astype(BF16), wb_ref[...])
    x = _ln_rows(ALPHA * r_ref[...] + mix, g_ref[...], beta_ref[...])
    x_ref[...] = x
    xh = x.astype(BF16)
    xl = (x - xh.astype(F32)).astype(BF16)
    lg_ref[...] = (_dot(xh, wrh_ref[...]) + _dot(xl, wrh_ref[...]) + _dot(xh, wrl_ref[...])
                   + br_ref[...])


def _mix_ln_router(r, a, b, wa, wb, g, beta, wr, br, *, tm):
    m, d = r.shape
    tm = _row_tile(m, tm)
    ka, kb = a.shape[1], b.shape[1]
    ne = wr.shape[1]
    wr_p = jnp.zeros((d, 128), F32).at[:, :ne].set(wr)
    wrh = wr_p.astype(BF16)
    wrl = (wr_p - wrh.astype(F32)).astype(BF16)
    br_p = jnp.zeros((1, 128), F32).at[0, :ne].set(br)
    row = lambda i: (i, 0)
    fix = lambda i: (0, 0)
    x, lg = pl.pallas_call(
        _mix_ln_router_kernel,
        grid=(m // tm,),
        in_specs=[pl.BlockSpec((tm, d), row), pl.BlockSpec((tm, ka), row), pl.BlockSpec((tm, kb), row),
                  pl.BlockSpec((ka, d), fix), pl.BlockSpec((kb, d), fix),
                  pl.BlockSpec((1, d), fix), pl.BlockSpec((1, d), fix),
                  pl.BlockSpec((d, 128), fix), pl.BlockSpec((d, 128), fix), pl.BlockSpec((1, 128), fix)],
        out_specs=[pl.BlockSpec((tm, d), row), pl.BlockSpec((tm, 128), row)],
        out_shape=[jax.ShapeDtypeStruct((m, d), F32), jax.ShapeDtypeStruct((m, 128), F32)],
        compiler_params=_cparams(("arbitrary",)),
        name="mix_ln_router",
    )(r, a, b, wa, wb, g.reshape(1, d), beta.reshape(1, d), wrh, wrl, br_p)
    return x, lg[:, :ne]


def _post_kernel(x1_ref, *refs):
    ys_refs = refs[:TOP_K]
    gt_ref, p_ref, wp_ref, wg_ref, g_ref, beta_ref, o_ref = refs[TOP_K:]
    y = gt_ref[:, 0:1] * ys_refs[0][...]
    for k in range(1, TOP_K):
        y = y + gt_ref[:, k:k + 1] * ys_refs[k][...]
    x = _ln_rows(ALPHA * x1_ref[...] + y, g_ref[...], beta_ref[...])
    proj = _dot(p_ref[...].astype(BF16), wp_ref[...])
    gate = _sigmoid(_dot(x.astype(BF16), wg_ref[...]))
    o_ref[...] = x + proj * gate


def _post(x1, ysel, gates, p, wp, wg, g, beta, *, tm):
    m, d = x1.shape
    tm = _row_tile(m, tm)
    nb = m // tm
    kp = p.shape[1]
    gt = jnp.zeros((m, 128), F32).at[:, :TOP_K].set(gates)
    row = lambda i: (i, 0)
    fix = lambda i: (0, 0)
    ys_specs = [pl.BlockSpec((tm, d), functools.partial(lambda i, k: (k * nb + i, 0), k=k)) for k in range(TOP_K)]
    return pl.pallas_call(
        _post_kernel,
        grid=(nb,),
        in_specs=[pl.BlockSpec((tm, d), row)] + ys_specs + [
            pl.BlockSpec((tm, 128), row), pl.BlockSpec((tm, kp), row), pl.BlockSpec((kp, d), fix),
            pl.BlockSpec((d, d), fix), pl.BlockSpec((1, d), fix), pl.BlockSpec((1, d), fix)],
        out_specs=pl.BlockSpec((tm, d), row),
        out_shape=jax.ShapeDtypeStruct((m, d), F32),
        compiler_params=_cparams(("arbitrary",)),
        name="post",
    )(x1, *([ysel] * TOP_K), gt, p, wp, wg, g.reshape(1, d), beta.reshape(1, d))


def _s5_discretize(lam_re, lam_im, b_re, b_im, log_dt):
    dt = jnp.exp(log_dt)[:, None]
    lr = jnp.minimum(lam_re, -1e-4)
    li = lam_im
    mag = jnp.exp(lr * dt)
    ab_re, ab_im = mag * jnp.cos(li * dt), mag * jnp.sin(li * dt)
    nr = ab_re - 1.0
    den = lr * lr + li * li
    zr = (nr * lr + ab_im * li) / den
    zi = (ab_im * lr - nr * li) / den
    bb_re = zr[..., None] * b_re - zi[..., None] * b_im
    bb_im = zr[..., None] * b_im + zi[..., None] * b_re
    return lr * dt, li * dt, ab_re, ab_im, bb_re, bb_im


def _s5_chunk_operators(lam_re, lam_im, b_re, b_im, c_re, c_im, log_dt):
    L = S5_CHUNK
    G, P, H = S5_GROUPS, S5_STATE, S5_GROUP
    ldr, ldi, _, _, bb_re, bb_im = _s5_discretize(lam_re, lam_im, b_re, b_im, log_dt)
    k = jnp.arange(L + 1, dtype=F32)[:, None, None]
    pk_mag = jnp.exp(k * ldr)
    pk_re, pk_im = pk_mag * jnp.cos(k * ldi), pk_mag * jnp.sin(k * ldi)
    abb_re = pk_re[..., None] * bb_re - pk_im[..., None] * bb_im
    abb_im = pk_re[..., None] * bb_im + pk_im[..., None] * bb_re
    kern = (jnp.einsum('gcp,kgpd->kgcd', c_re, abb_re[:L]) - jnp.einsum('gcp,kgpd->kgcd', c_im, abb_im[:L]))
    t = jnp.arange(L)
    lag = t[None, :] - t[:, None]
    toe = jnp.where((lag >= 0)[:, :, None, None, None], kern[jnp.clip(lag, 0, L - 1)], 0.0)
    m_op = jnp.transpose(toe, (2, 0, 4, 1, 3)).reshape(G, L * H, L * H)
    rev_re = abb_re[L - 1 - t]
    rev_im = abb_im[L - 1 - t]
    bs_op = jnp.concatenate([jnp.transpose(rev_re, (1, 0, 3, 2)).reshape(G, L * H, P),
                             jnp.transpose(rev_im, (1, 0, 3, 2)).reshape(G, L * H, P)], -1)
    ca_re = c_re[None] * pk_re[1:, :, None, :] - c_im[None] * pk_im[1:, :, None, :]
    ca_im = c_re[None] * pk_im[1:, :, None, :] + c_im[None] * pk_re[1:, :, None, :]
    cs_op = jnp.concatenate([jnp.transpose(ca_re, (1, 3, 0, 2)).reshape(G, P, L * H),
                             -jnp.transpose(ca_im, (1, 3, 0, 2)).reshape(G, P, L * H)], 1)
    return m_op, bs_op, cs_op, pk_re[L], pk_im[L]


def _s5_state_kernel(u_ref, bs_ref, s_ref, *, gpb):
    for q in range(gpb):
        s_ref[:, q * 128:(q + 1) * 128] = _dot(u_ref[q].astype(BF16), bs_ref[q])


def _s5_scan_kernel(sr_ref, si_ref, ar_ref, ai_ref, x0r_ref, x0i_ref, er_ref, ei_ref):
    nchunk = sr_ref.shape[0]
    ar, ai = ar_ref[...], ai_ref[...]

    def body(j, carry):
        xr, xi = carry
        x0r_ref[j] = xr
        x0i_ref[j] = xi
        return ar * xr - ai * xi + sr_ref[j], ar * xi + ai * xr + si_ref[j]

    zero = jnp.zeros(sr_ref.shape[1:], F32)
    xr, xi = lax.fori_loop(0, nchunk, body, (zero, zero))
    er_ref[...] = xr
    ei_ref[...] = xi


def _s5_out_kernel(u_ref, x0_ref, m_ref, cs_ref, d_ref, o_ref, *, gpb):
    for q in range(gpb):
        u = u_ref[q]
        y = (_dot(u.astype(BF16), m_ref[q]) + _dot(x0_ref[:, q * 128:(q + 1) * 128].astype(BF16), cs_ref[q])
             + d_ref[q] * u)
        o_ref[q] = _gelu(y)


def _s5_prompt(u, ops, d_skip):
    m_op, bs_op, cs_op, al_re, al_im = ops
    bsz, t, _ = u.shape
    L, G, P, H = S5_CHUNK, S5_GROUPS, S5_STATE, S5_GROUP
    nchunk = t // L
    rows = nchunk * bsz
    gpb = 8
    ug = jnp.transpose(u.reshape(bsz, nchunk, L, G, H), (3, 1, 0, 2, 4)).reshape(G, rows, L * H)
    s_loc = pl.pallas_call(
        functools.partial(_s5_state_kernel, gpb=gpb),
        grid=(G // gpb,),
        in_specs=[pl.BlockSpec((gpb, rows, L * H), lambda g: (g, 0, 0)),
                  pl.BlockSpec((gpb, L * H, 2 * P), lambda g: (g, 0, 0))],
        out_specs=pl.BlockSpec((rows, gpb * 2 * P), lambda g: (0, g)),
        out_shape=jax.ShapeDtypeStruct((rows, G * 2 * P), F32),
        compiler_params=_cparams(("arbitrary",)),
        name="s5_state",
    )(ug, bs_op.astype(BF16))
    s4 = s_loc.reshape(nchunk, bsz, G, 2, P)
    sr = s4[:, :, :, 0].reshape(nchunk, bsz, G * P)
    si = s4[:, :, :, 1].reshape(nchunk, bsz, G * P)
    lw = 512
    seq = pl.BlockSpec((nchunk, bsz, lw), lambda i: (0, 0, i))
    vec = pl.BlockSpec((1, lw), lambda i: (0, i))
    fin = pl.BlockSpec((bsz, lw), lambda i: (0, i))
    x0r, x0i, er, ei = pl.pallas_call(
        _s5_scan_kernel,
        grid=(G * P // lw,),
        in_specs=[seq, seq, vec, vec],
        out_specs=[seq, seq, fin, fin],
        out_shape=[jax.ShapeDtypeStruct((nchunk, bsz, G * P), F32)] * 2
        + [jax.ShapeDtypeStruct((bsz, G * P), F32)] * 2,
        compiler_params=_cparams(("arbitrary",)),
        name="s5_scan",
    )(sr, si, al_re.reshape(1, G * P), al_im.reshape(1, G * P))
    x0 = jnp.stack([x0r.reshape(rows, G, P), x0i.reshape(rows, G, P)], 2).reshape(rows, G * 2 * P)
    d_t = jnp.tile(d_skip, (1, L)).reshape(G, 1, L * H)
    g5 = pl.pallas_call(
        functools.partial(_s5_out_kernel, gpb=gpb),
        grid=(G // gpb,),
        in_specs=[pl.BlockSpec((gpb, rows, L * H), lambda g: (g, 0, 0)),
                  pl.BlockSpec((rows, gpb * 2 * P), lambda g: (0, g)),
                  pl.BlockSpec((gpb, L * H, L * H), lambda g: (g, 0, 0)),
                  pl.BlockSpec((gpb, 2 * P, L * H), lambda g: (g, 0, 0)),
                  pl.BlockSpec((gpb, 1, L * H), lambda g: (g, 0, 0))],
        out_specs=pl.BlockSpec((gpb, rows, L * H), lambda g: (g, 0, 0)),
        out_shape=jax.ShapeDtypeStruct((G, rows, L * H), F32),
        compiler_params=_cparams(("arbitrary",)),
        name="s5_out",
    )(ug, x0, m_op.astype(BF16), cs_op.astype(BF16), d_t)
    g5 = jnp.transpose(g5.reshape(G, nchunk, bsz, L, H), (2, 1, 3, 0, 4)).reshape(bsz, t, G * H)
    return g5, er.reshape(bsz, G, P), ei.reshape(bsz, G, P)


def _s5_step_kernel(u_ref, x0r_ref, x0i_ref, ar_ref, ai_ref, bbr_ref, bbi_ref, cr_ref, ci_ref, d_ref,
                    g_ref, xr_ref, xi_ref, *, nblk, bw, sw):
    for q in range(nblk):
        u = u_ref[:, q * bw:(q + 1) * bw]
        ub = u.astype(BF16)
        sl = slice(q * sw, (q + 1) * sw)
        ar, ai = ar_ref[:, sl], ai_ref[:, sl]
        x0r, x0i = x0r_ref[:, sl], x0i_ref[:, sl]
        xr = ar * x0r - ai * x0i + _dot(ub, bbr_ref[q])
        xi = ar * x0i + ai * x0r + _dot(ub, bbi_ref[q])
        xr_ref[:, sl] = xr
        xi_ref[:, sl] = xi
        y = _dot(xr.astype(BF16), cr_ref[q]) - _dot(xi.astype(BF16), ci_ref[q]) + d_ref[:, q * bw:(q + 1) * bw] * u
        g_ref[:, q * bw:(q + 1) * bw] = _gelu(y)


def _s5_step(u, x0r, x0i, lam_re, lam_im, b_re, b_im, c_re, c_im, log_dt, d_skip):
    n = u.shape[0]
    G, P, H = S5_GROUPS, S5_STATE, S5_GROUP
    gb = 16
    nblk = G // gb
    _, _, ab_re, ab_im, bb_re, bb_im = _s5_discretize(lam_re, lam_im, b_re, b_im, log_dt)
    eye = jnp.eye(gb, dtype=F32)

    def in_blocks(w):
        w4 = w.reshape(nblk, gb, P, H)
        return jnp.einsum('qgph,gk->qghkp', w4, eye).reshape(nblk, gb * H, gb * P)

    def out_blocks(w):
        w4 = w.reshape(nblk, gb, H, P)
        return jnp.einsum('qghp,gk->qgpkh', w4, eye).reshape(nblk, gb * P, gb * H)

    full = lambda i: (0, 0)
    full3 = lambda i: (0, 0, 0)
    g5, xr, xi = pl.pallas_call(
        functools.partial(_s5_step_kernel, nblk=nblk, bw=gb * H, sw=gb * P),
        grid=(1,),
        in_specs=[pl.BlockSpec((n, G * H), full), pl.BlockSpec((n, G * P), full), pl.BlockSpec((n, G * P), full),
                  pl.BlockSpec((1, G * P), full), pl.BlockSpec((1, G * P), full),
                  pl.BlockSpec((nblk, gb * H, gb * P), full3), pl.BlockSpec((nblk, gb * H, gb * P), full3),
                  pl.BlockSpec((nblk, gb * P, gb * H), full3), pl.BlockSpec((nblk, gb * P, gb * H), full3),
                  pl.BlockSpec((1, G * H), full)],
        out_specs=[pl.BlockSpec((n, G * H), full), pl.BlockSpec((n, G * P), full), pl.BlockSpec((n, G * P), full)],
        out_shape=[jax.ShapeDtypeStruct((n, G * H), F32), jax.ShapeDtypeStruct((n, G * P), F32),
                   jax.ShapeDtypeStruct((n, G * P), F32)],
        compiler_params=_cparams(("arbitrary",)),
        name="s5_step",
    )(u, x0r.reshape(n, G * P), x0i.reshape(n, G * P), ab_re.reshape(1, G * P), ab_im.reshape(1, G * P),
      in_blocks(bb_re).astype(BF16), in_blocks(bb_im).astype(BF16),
      out_blocks(c_re).astype(BF16), out_blocks(c_im).astype(BF16), d_skip.reshape(1, G * H))
    return g5, xr.reshape(n, G, P), xi.reshape(n, G, P)


def _sg_kernel(u_ref, v_ref, g_ref, b_ref, w_ref, bias_ref, o_ref, vl_ref, *, cpb):
    i = pl.program_id(0)
    v = _ln_rows(v_ref[...], g_ref[...], b_ref[...])
    vb = v.astype(BF16)
    hd = SG_WIDTH // SG_HEADS
    for h in range(SG_HEADS):
        sl = slice(h * hd, (h + 1) * hd)
        s = _dot(w_ref[h], vb[:, sl]) + bias_ref[:, sl]
        o_ref[:, sl] = u_ref[:, sl] * s

    @pl.when(i % cpb == cpb - 1)
    def _():
        vl_ref[...] = v


def _spatial_gating_prompt(z, col0, bsz, t, ln_g, ln_b, w_s, b_s):
    cpb = t // SG_CHUNK
    cu = col0 // SG_WIDTH
    w = (w_s * jnp.tril(jnp.ones((SG_CHUNK, SG_CHUNK), F32))).astype(BF16)
    bias = jnp.repeat(b_s.T, SG_WIDTH // SG_HEADS, axis=1)
    fix = lambda i: (0, 0)
    return pl.pallas_call(
        functools.partial(_sg_kernel, cpb=cpb),
        grid=(bsz * cpb,),
        in_specs=[pl.BlockSpec((SG_CHUNK, SG_WIDTH), lambda i: (i, cu)),
                  pl.BlockSpec((SG_CHUNK, SG_WIDTH), lambda i: (i, cu + 1)),
                  pl.BlockSpec((1, SG_WIDTH), fix), pl.BlockSpec((1, SG_WIDTH), fix),
                  pl.BlockSpec((SG_HEADS, SG_CHUNK, SG_CHUNK), lambda i: (0, 0, 0)),
                  pl.BlockSpec((SG_CHUNK, SG_WIDTH), fix)],
        out_specs=[pl.BlockSpec((SG_CHUNK, SG_WIDTH), lambda i: (i, 0)),
                   pl.BlockSpec((None, SG_CHUNK, SG_WIDTH), lambda i: (i // cpb, 0, 0))],
        out_shape=[jax.ShapeDtypeStruct((bsz * t, SG_WIDTH), F32),
                   jax.ShapeDtypeStruct((bsz, SG_CHUNK, SG_WIDTH), F32)],
        compiler_params=_cparams(("arbitrary",)),
        name="spatial_gating",
    )(z, z, ln_g.reshape(1, -1), ln_b.reshape(1, -1), w, bias)


def _sg_step_kernel(u_ref, v_ref, g_ref, b_ref, w0_ref, b0_ref, o_ref, vn_ref):
    v = _ln_rows(v_ref[...], g_ref[...], b_ref[...])
    vn_ref[...] = v
    o_ref[...] = u_ref[...] * (w0_ref[...] * v + b0_ref[...])


def _spatial_gating_step(zs, ln_g, ln_b, w_s, b_s):
    n = zs.shape[0]
    hd = SG_WIDTH // SG_HEADS
    w0 = jnp.repeat(w_s[:, 0, 0], hd).reshape(1, SG_WIDTH)
    b0 = jnp.repeat(b_s[:, 0], hd).reshape(1, SG_WIDTH)
    fix = lambda i: (0, 0)
    return pl.pallas_call(
        _sg_step_kernel,
        grid=(1,),
        in_specs=[pl.BlockSpec((n, SG_WIDTH), lambda i: (0, 0)), pl.BlockSpec((n, SG_WIDTH), lambda i: (0, 1)),
                  pl.BlockSpec((1, SG_WIDTH), fix), pl.BlockSpec((1, SG_WIDTH), fix),
                  pl.BlockSpec((1, SG_WIDTH), fix), pl.BlockSpec((1, SG_WIDTH), fix)],
        out_specs=[pl.BlockSpec((n, SG_WIDTH), fix), pl.BlockSpec((n, SG_WIDTH), fix)],
        out_shape=[jax.ShapeDtypeStruct((n, SG_WIDTH), F32)] * 2,
        compiler_params=_cparams(("arbitrary",)),
        name="spatial_gating_step",
    )(zs, zs, ln_g.reshape(1, -1), ln_b.reshape(1, -1), w0, b0)


def _online_softmax_update(s, m_sc, l_sc):
    m_old = m_sc[...]
    m_new = jnp.maximum(m_old, jnp.max(s, -1, keepdims=True))
    a = jnp.exp(m_old - m_new)
    p = jnp.exp(s - m_new)
    l_sc[...] = a * l_sc[...] + jnp.sum(p, -1, keepdims=True)
    m_sc[...] = m_new
    return a, p


def _causal_mask(s, qi, kb, tk):
    row = lax.broadcasted_iota(jnp.int32, s.shape, 0)
    col = lax.broadcasted_iota(jnp.int32, s.shape, 1)
    qpos = qi * Q_BLOCK + (row & (Q_BLOCK - 1))
    return jnp.where(qpos >= kb * tk + col, s, NEG_INF)


def _key_block_cases(qi, kb, tk, step):
    needed = kb * tk <= qi * Q_BLOCK + Q_BLOCK - 1
    crosses = (kb + 1) * tk - 1 > qi * Q_BLOCK

    @pl.when(needed & crosses)
    def _():
        step(True)

    @pl.when(needed & jnp.logical_not(crosses))
    def _():
        step(False)


def _mla_prefill_kernel(ql_ref, qp_ref, ckv_ref, kpe_ref, wuv_ref, y_ref, m_sc, l_sc, acc_sc, *, tk):
    qi, kb = pl.program_id(1), pl.program_id(2)

    @pl.when(kb == 0)
    def _():
        m_sc[...] = jnp.full(m_sc.shape, NEG_INF, F32)
        l_sc[...] = jnp.zeros(l_sc.shape, F32)
        acc_sc[...] = jnp.zeros(acc_sc.shape, F32)

    def step(masked):
        k1 = ckv_ref[...].astype(BF16)
        s = _dot_nt(ql_ref[...], k1) + _dot_nt(qp_ref[...], kpe_ref[...].astype(BF16))
        if masked:
            s = _causal_mask(s, qi, kb, tk)
        a, p = _online_softmax_update(s, m_sc, l_sc)
        acc_sc[...] = a * acc_sc[...] + _dot(p.astype(BF16), k1)

    _key_block_cases(qi, kb, tk, step)

    @pl.when(kb == pl.num_programs(2) - 1)
    def _():
        o = (acc_sc[...] / l_sc[...]).astype(BF16)
        for h in range(MLA_HEADS):
            y_ref[:, h * MLA_V:(h + 1) * MLA_V] = _dot(o[h * Q_BLOCK:(h + 1) * Q_BLOCK],
                                                        wuv_ref[:, h * MLA_V:(h + 1) * MLA_V])


def _mla_prefill(q_lat, q_pe, c_kv, k_pe, w_uv, *, tk=512):
    bsz, t, nh, r = q_lat.shape
    nq = t // Q_BLOCK
    tk = min(tk, t)
    rows = nh * Q_BLOCK

    scale = (MLA_NOPE + MLA_ROPE) ** -0.5

    def blocks(q):
        d = q.shape[-1]
        q = jnp.transpose((q * scale).reshape(bsz, nq, Q_BLOCK, nh, d), (0, 1, 3, 2, 4))
        return q.reshape(bsz, nq, rows, d).astype(BF16)

    kmap = lambda b, qi, kb: (b, jnp.minimum(kb, (qi * Q_BLOCK + Q_BLOCK - 1) // tk), 0)
    return pl.pallas_call(
        functools.partial(_mla_prefill_kernel, tk=tk),
        grid=(bsz, nq, t // tk),
        in_specs=[pl.BlockSpec((None, None, rows, r), lambda b, qi, kb: (b, qi, 0, 0)),
                  pl.BlockSpec((None, None, rows, MLA_ROPE), lambda b, qi, kb: (b, qi, 0, 0)),
                  pl.BlockSpec((None, tk, r), kmap),
                  pl.BlockSpec((None, tk, MLA_ROPE), kmap),
                  pl.BlockSpec((r, nh * MLA_V), lambda b, qi, kb: (0, 0))],
        out_specs=pl.BlockSpec((None, Q_BLOCK, nh * MLA_V), lambda b, qi, kb: (b, qi, 0)),
        out_shape=jax.ShapeDtypeStruct((bsz, t, nh * MLA_V), F32),
        scratch_shapes=[pltpu.VMEM((rows, 1), F32), pltpu.VMEM((rows, 1), F32), pltpu.VMEM((rows, r), F32)],
        compiler_params=_cparams(("arbitrary", "arbitrary", "arbitrary")),
        name="mla_prefill",
    )(blocks(q_lat), blocks(q_pe), c_kv, k_pe, w_uv.reshape(r, nh * MLA_V).astype(BF16))


def _diff_finalize(o0, o1, lam, subln, lam_init):
    o = o0 - lam * o1
    ms = jnp.mean(o * o, -1, keepdims=True)
    return o * lax.rsqrt(ms + 1e-5) * subln * (1.0 - lam_init)


def _diff_prefill_kernel(q_ref, k_ref, v_ref, lam_ref, sub_ref, y_ref, m_sc, l_sc, acc_sc, *, tk, lam_init):
    qi, kb = pl.program_id(1), pl.program_id(2)

    @pl.when(kb == 0)
    def _():
        m_sc[...] = jnp.full(m_sc.shape, NEG_INF, F32)
        l_sc[...] = jnp.zeros(l_sc.shape, F32)
        acc_sc[...] = jnp.zeros(acc_sc.shape, F32)

    def step(masked):
        vb = v_ref[...].astype(BF16)
        for mi in range(2):
            s = _dot_nt(q_ref[mi], k_ref[mi].astype(BF16))
            if masked:
                s = _causal_mask(s, qi, kb, tk)
            a, p = _online_softmax_update(s, m_sc.at[mi], l_sc.at[mi])
            acc_sc[mi] = a * acc_sc[mi] + _dot(p.astype(BF16), vb)

    _key_block_cases(qi, kb, tk, step)

    @pl.when(kb == pl.num_programs(2) - 1)
    def _():
        o = _diff_finalize(acc_sc[0] / l_sc[0], acc_sc[1] / l_sc[1], lam_ref[...], sub_ref[...], lam_init)
        for g in range(DIFF_GROUP):
            y_ref[:, g * DIFF_V_DIM:(g + 1) * DIFF_V_DIM] = o[g * Q_BLOCK:(g + 1) * Q_BLOCK]


def _diff_prefill(dq, dk, dv, lam, subln, lam_init, *, tk=512):
    bsz, t = dq.shape[:2]
    nq = t // Q_BLOCK
    tk = min(tk, t)
    rows = DIFF_GROUP * Q_BLOCK
    d = DIFF_HEAD_DIM
    q = (dq * d ** -0.5).reshape(bsz, nq, Q_BLOCK, DIFF_KV_HEADS, DIFF_GROUP, 2, d)
    q = jnp.transpose(q, (0, 3, 1, 5, 4, 2, 6))
    q = q.reshape(bsz * DIFF_KV_HEADS, nq, 2, rows, d).astype(BF16)
    k = jnp.transpose(dk, (0, 2, 3, 1, 4)).reshape(bsz * DIFF_KV_HEADS, 2, t, d)
    v = jnp.transpose(dv, (0, 2, 1, 3)).reshape(bsz * DIFF_KV_HEADS, t, DIFF_V_DIM)
    last = lambda qi: (qi * Q_BLOCK + Q_BLOCK - 1) // tk
    width = DIFF_GROUP * DIFF_V_DIM
    return pl.pallas_call(
        functools.partial(_diff_prefill_kernel, tk=tk, lam_init=lam_init),
        grid=(bsz * DIFF_KV_HEADS, nq, t // tk),
        in_specs=[pl.BlockSpec((None, None, 2, rows, d), lambda a, qi, kb: (a, qi, 0, 0, 0)),
                  pl.BlockSpec((None, 2, tk, d), lambda a, qi, kb: (a, 0, jnp.minimum(kb, last(qi)), 0)),
                  pl.BlockSpec((None, tk, DIFF_V_DIM), lambda a, qi, kb: (a, jnp.minimum(kb, last(qi)), 0)),
                  pl.BlockSpec((1, DIFF_V_DIM), lambda a, qi, kb: (0, 0)),
                  pl.BlockSpec((1, DIFF_V_DIM), lambda a, qi, kb: (0, 0))],
        out_specs=pl.BlockSpec((None, Q_BLOCK, width),
                               lambda a, qi, kb: (a // DIFF_KV_HEADS, qi, a % DIFF_KV_HEADS)),
        out_shape=jax.ShapeDtypeStruct((bsz, t, DIFF_KV_HEADS * width), F32),
        scratch_shapes=[pltpu.VMEM((2, rows, 1), F32), pltpu.VMEM((2, rows, 1), F32),
                        pltpu.VMEM((2, rows, DIFF_V_DIM), F32)],
        compiler_params=_cparams(("arbitrary", "arbitrary", "arbitrary")),
        name="diff_prefill",
    )(q, k, v, jnp.full((1, DIFF_V_DIM), lam, F32), subln.reshape(1, DIFF_V_DIM))


def _decode_init(s0_ref, v0_ref, m_sc, l_sc, acc_sc):
    m_sc[...] = s0_ref[...]
    l_sc[...] = jnp.ones(l_sc.shape, F32)
    acc_sc[...] = jnp.broadcast_to(v0_ref[...], acc_sc.shape)


def _mla_decode_step(small, pages, m_sc, l_sc, acc_sc, *, npg, scale):
    q1_ref, q2_ref = small[0], small[1]
    lat_ref, pet_ref = pages
    q1, q2 = q1_ref[...], q2_ref[...]
    lats, scores = [], []
    for i in range(npg):
        lat = lat_ref[i].astype(BF16)
        scores.append(_dot_nt(q1, lat) + _dot(q2, pet_ref[i].astype(BF16)))
        lats.append(lat)
    s = jnp.concatenate(scores, axis=1) * scale
    a, p = _online_softmax_update(s, m_sc, l_sc)
    pb = p.astype(BF16)
    pw = scores[0].shape[1]
    acc = a * acc_sc[...]
    for i in range(npg):
        acc = acc + _dot(pb[:, i * pw:(i + 1) * pw], lats[i])
    acc_sc[...] = acc


def _diff_decode_step(small, pages, m_sc, l_sc, acc_sc, *, npg, page, scale):
    q = small[0][...]
    kt_ref, v_ref = pages
    scores = [_dot(q, kt_ref[i].astype(BF16)) for i in range(npg)]
    s = jnp.concatenate(scores, axis=1) * scale
    a, p = _online_softmax_update(s, m_sc, l_sc)
    pb = p.astype(BF16)
    for h in range(DIFF_KV_HEADS):
        sl = slice(h * DIFF_V_DIM, (h + 1) * DIFF_V_DIM)
        acc = a * acc_sc[:, sl]
        for i in range(npg):
            vh = v_ref[i, pl.ds(h, page, stride=DIFF_KV_HEADS), :].astype(BF16)
            acc = acc + _dot(pb[:, i * page:(i + 1) * page], vh)
        acc_sc[:, sl] = acc


def _paged_kernel(pt_ref, *refs, n_small, n_pools, npg, step_fn):
    small = refs[:n_small]
    pools = refs[n_small:n_small + n_pools]
    o_ref = refs[n_small + n_pools]
    bufs = refs[n_small + n_pools + 1:n_small + 2 * n_pools + 1]
    sem, m_sc, l_sc, acc_sc = refs[n_small + 2 * n_pools + 1:]
    b, sg = pl.program_id(0), pl.program_id(1)
    nb, ng = pl.num_programs(0), pl.num_programs(1)

    def page_copy(k, i, slot, pid):
        return pltpu.make_async_copy(pools[k].at[pid], bufs[k].at[slot, i], sem.at[k, slot])

    def start_group(bb, grp, slot):
        for k in range(n_pools):
            for i in range(npg):
                page_copy(k, i, slot, pt_ref[bb, grp * npg + i]).start()

    def wait_group(slot):
        for k in range(n_pools):
            for i in range(npg):
                page_copy(k, i, slot, 0).wait()

    @pl.when((b == 0) & (sg == 0))
    def _():
        start_group(0, 0, 0)

    @pl.when(sg == 0)
    def _():
        _decode_init(small[-2], small[-1], m_sc, l_sc, acc_sc)

    start_group(b, 2 * sg + 1, 1)
    wait_group(0)
    step_fn(small, [buf.at[0] for buf in bufs], m_sc, l_sc, acc_sc)

    row_done = sg == ng - 1

    @pl.when(jnp.logical_not(row_done & (b == nb - 1)))
    def _():
        start_group(jnp.where(row_done, b + 1, b), jnp.where(row_done, 0, 2 * sg + 2), 0)

    wait_group(1)
    step_fn(small, [buf.at[1] for buf in bufs], m_sc, l_sc, acc_sc)

    @pl.when(row_done)
    def _():
        o_ref[...] = acc_sc[...] / l_sc[...]


def _paged_call(step_fn, name, page_table, small, pools, r, dv):
    bsz, npages = page_table.shape
    assert npages % 2 == 0
    npg = math.gcd(PAGES_PER_STEP, npages // 2)
    per_b = lambda b, s, pt: (b, 0, 0)
    in_specs = [pl.BlockSpec((None,) + a.shape[1:], per_b) for a in small]
    in_specs += [pl.BlockSpec(memory_space=pl.ANY) for _ in pools]
    grid_spec = pltpu.PrefetchScalarGridSpec(
        num_scalar_prefetch=1,
        grid=(bsz, npages // (2 * npg)),
        in_specs=in_specs,
        out_specs=pl.BlockSpec((None, r, dv), per_b),
        scratch_shapes=[pltpu.VMEM((2, npg) + pool.shape[1:], pool.dtype) for pool in pools]
        + [pltpu.SemaphoreType.DMA((len(pools), 2)),
           pltpu.VMEM((r, 1), F32), pltpu.VMEM((r, 1), F32), pltpu.VMEM((r, dv), F32)],
    )
    return pl.pallas_call(
        functools.partial(_paged_kernel, n_small=len(small), n_pools=len(pools), npg=npg,
                          step_fn=functools.partial(step_fn, npg=npg)),
        grid_spec=grid_spec,
        out_shape=jax.ShapeDtypeStruct((bsz, r, dv), F32),
        compiler_params=_cparams(("arbitrary", "arbitrary")),
        name=name,
    )(page_table, *small, *pools)


def _moe_kernel(first_ref, e_ref, nsub_ref, oblk_ref, *refs):
    del first_ref, e_ref, oblk_ref
    x_refs = refs[:MOE_NSUB]
    wg_ref, bg_ref, wu_ref, bu_ref, wd_ref, bd_ref, o_ref, wg_sc, wu_sc, wd_sc = refs[MOE_NSUB:]
    s, j = pl.program_id(0), pl.program_id(1)
    ns = nsub_ref[s]

    @pl.when((ns == 0) & (j == 0))
    def _():
        o_ref[...] = jnp.zeros(o_ref.shape, F32)

    @pl.when(ns > 0)
    def _():
        @pl.when(j == 0)
        def _():
            o_ref[...] = jnp.broadcast_to(bd_ref[...], o_ref.shape)

        for cnt in range(1, MOE_NSUB + 1):
            @pl.when(ns == cnt)
            def _(cnt=cnt):
                wg_sc[...] = wg_ref[...].astype(BF16)
                wu_sc[...] = wu_ref[...].astype(BF16)
                wd_sc[...] = wd_ref[...].astype(BF16)
                for i in range(cnt):
                    xb = x_refs[i][...].astype(BF16)
                    g = jnp.minimum(_dot(xb, wg_sc[...]) + bg_ref[...], SWIGLU_LIMIT)
                    u = jnp.clip(_dot(xb, wu_sc[...]) + bu_ref[...], -SWIGLU_LIMIT, SWIGLU_LIMIT)
                    h = (u + 1.0) * (g * _sigmoid(SWIGLU_ALPHA * g))
                    o_ref[i * MOE_SUB:(i + 1) * MOE_SUB, :] += _dot(h.astype(BF16), wd_sc[...])


def _moe(x, logits, layer, wg, bg, wu, bu, wd, bd):
    n, d = x.shape
    f = wg.shape[-1]
    sub, nsub_max, tf = MOE_SUB, MOE_NSUB, MOE_TF
    tm = sub * nsub_max
    nf = f // tf
    nk = n * TOP_K
    nblk = nk // sub + N_EXPERTS
    nsb = (nblk + (nsub_max - 1) * N_EXPERTS) // nsub_max
    top_v, top_i = lax.top_k(logits, TOP_K)
    gates = jax.nn.softmax(top_v, axis=-1)
    flat_e = top_i.reshape(-1).astype(I32)
    order = jnp.argsort(flat_e).astype(I32)
    inv = jnp.argsort(order).astype(I32)
    eids = jnp.arange(N_EXPERTS, dtype=I32)
    counts = jnp.sum((flat_e[:, None] == eids[None, :]).astype(I32), axis=0)
    grp_start = (jnp.cumsum(counts) - counts).astype(I32)
    blocks_e = (counts + sub - 1) // sub
    blk_end = jnp.cumsum(blocks_e).astype(I32)
    blk_start = blk_end - blocks_e
    sb_per_e = (blocks_e + nsub_max - 1) // nsub_max
    sb_end = jnp.cumsum(sb_per_e).astype(I32)
    sb_start = sb_end - sb_per_e
    r = jnp.arange(nblk * sub, dtype=I32)
    e_r = jnp.minimum(jnp.sum((r[:, None] // sub >= blk_end[None, :]).astype(I32), axis=1), N_EXPERTS - 1)
    rank_r = r - blk_start[e_r] * sub
    valid_r = (r // sub < blk_end[-1]) & (rank_r < counts[e_r])
    tok = jnp.where(valid_r, order[jnp.clip(grp_start[e_r] + rank_r, 0, nk - 1)] // TOP_K, 0)
    xs = x[tok]
    n_used = sb_end[-1]
    sb_ids = jnp.arange(nsb, dtype=I32)
    sbc = jnp.minimum(sb_ids, n_used - 1)
    sb_e = jnp.minimum(jnp.sum((sbc[:, None] >= sb_end[None, :]).astype(I32), axis=1), N_EXPERTS - 1)
    kk = sbc - sb_start[sb_e]
    first = blk_start[sb_e] + kk * nsub_max
    nsub = jnp.where(sb_ids < n_used, jnp.clip(blocks_e[sb_e] - kk * nsub_max, 0, nsub_max), 0).astype(I32)

    jj = lambda s, j, ns: jnp.where(ns[s] > 0, j, nf - 1)

    def x_spec(i):
        return pl.BlockSpec((sub, d), lambda s, j, fr, e, ns, ob: (jnp.minimum(fr[s] + i, nblk - 1), 0))

    grid_spec = pltpu.PrefetchScalarGridSpec(
        num_scalar_prefetch=4,
        grid=(nsb, nf),
        in_specs=[x_spec(i) for i in range(nsub_max)] + [
            pl.BlockSpec((None, None, d, tf), lambda s, j, fr, e, ns, ob: (layer, e[s], 0, jj(s, j, ns))),
            pl.BlockSpec((None, None, 1, tf), lambda s, j, fr, e, ns, ob: (layer, e[s], 0, jj(s, j, ns))),
            pl.BlockSpec((None, None, d, tf), lambda s, j, fr, e, ns, ob: (layer, e[s], 0, jj(s, j, ns))),
            pl.BlockSpec((None, None, 1, tf), lambda s, j, fr, e, ns, ob: (layer, e[s], 0, jj(s, j, ns))),
            pl.BlockSpec((None, None, tf, d), lambda s, j, fr, e, ns, ob: (layer, e[s], jj(s, j, ns), 0)),
            pl.BlockSpec((None, None, 1, d), lambda s, j, fr, e, ns, ob: (layer, e[s], 0, 0))],
        out_specs=pl.BlockSpec((tm, d), lambda s, j, fr, e, ns, ob: (ob[s], 0)),
        scratch_shapes=[pltpu.VMEM((d, tf), BF16), pltpu.VMEM((d, tf), BF16), pltpu.VMEM((tf, d), BF16)],
    )
    nl, ne = wg.shape[:2]
    yb = pl.pallas_call(
        _moe_kernel,
        grid_spec=grid_spec,
        out_shape=jax.ShapeDtypeStruct((nsb * tm, d), F32),
        compiler_params=_cparams(("arbitrary", "arbitrary")),
        name="moe_experts",
    )(first.astype(I32), sb_e.astype(I32), nsub, sb_ids, *([xs] * nsub_max),
      wg, bg.reshape(nl, ne, 1, f), wu, bu.reshape(nl, ne, 1, f), wd, bd.reshape(nl, ne, 1, d))
    rank_a = inv - grp_start[flat_e]
    dest = (sb_start[flat_e] + rank_a // tm) * tm + rank_a % tm
    return yb[dest.reshape(n, TOP_K).T.reshape(-1)], gates


def _rmsnorm(x, g, eps=1e-6):
    return x * lax.rsqrt(jnp.mean(x * x, -1, keepdims=True) + eps) * g


def _rope(x, pos):
    d = x.shape[-1]
    inv = ROPE_THETA ** (-jnp.arange(0, d, 2, dtype=F32) / d)
    ang = pos.astype(F32)[:, None] * inv[None, :]
    ang = ang.reshape((ang.shape[0],) + (1,) * (x.ndim - 2) + (ang.shape[1],))
    cos, sin = jnp.cos(ang), jnp.sin(ang)
    x1, x2 = jnp.split(x, 2, axis=-1)
    return jnp.concatenate([x1 * cos - x2 * sin, x1 * sin + x2 * cos], -1)


def _layer_ab(x, np_rows, bsz, t, state_re, state_im, w):
    (w_in, lam_re, lam_im, b_re, b_im, c_re, c_im, d_skip, log_dt, w_glu, b_glu,
     ln_g, ln_b, w_s, b_s) = w
    z = _mm(x, w_in.astype(BF16), tm=640, tn=512, gelu_from=S5_WIDTH // 512)
    ops = _s5_chunk_operators(lam_re, lam_im, b_re, b_im, c_re, c_im, log_dt)
    g5_p, pr, pi = _s5_prompt(z[:np_rows, :S5_WIDTH].reshape(bsz, t, S5_WIDTH), ops, d_skip)
    g5_s, sr, si = _s5_step(z[np_rows:, :S5_WIDTH], state_re, state_im,
                            lam_re, lam_im, b_re, b_im, c_re, c_im, log_dt, d_skip)
    g5 = jnp.concatenate([g5_p.reshape(np_rows, S5_WIDTH), g5_s], 0)
    o5 = _glu(g5, w_glu.astype(BF16), b_glu.reshape(1, -1), tm=640, tn=512)
    osg_p, v_p = _spatial_gating_prompt(z, S5_WIDTH, bsz, t, ln_g, ln_b, w_s, b_s)
    osg_s, v_s = _spatial_gating_step(z[np_rows:, S5_WIDTH:], ln_g, ln_b, w_s, b_s)
    osg = jnp.concatenate([osg_p, osg_s], 0)
    return o5, osg, (pr, pi, sr, si, v_p, v_s[:, None, :])


def _layer_cd(x, np_rows, bsz, t, pos, layer, caches, page_table, w):
    (w_in, q_norm, kv_norm, w_uq, w_uk, w_uv, lq1, lk1, lq2, lk2, subln) = w
    lat_pool, pet_pool, dkt_pool, dv_pool = caches
    n = x.shape[0]
    ns = n - np_rows
    nh = MLA_HEADS
    page = lat_pool.shape[1]
    a1, a2 = MLA_Q_RANK + MLA_KV_RANK, MLA_Q_RANK + MLA_KV_RANK + MLA_ROPE
    ndq = DIFF_HEADS * 2 * DIFF_HEAD_DIM
    ndk = DIFF_KV_HEADS * 2 * DIFF_HEAD_DIM
    ndv = DIFF_KV_HEADS * DIFF_V_DIM
    w_perm = jnp.concatenate([w_in[:, :a1], w_in[:, a2:], w_in[:, a1:a2],
                              jnp.zeros((w_in.shape[0], 128 - MLA_ROPE), F32)], 1).astype(BF16)
    z = _mm(x, w_perm, tm=640, tn=w_perm.shape[1])
    c = 0
    q_lat = z[:, c:c + MLA_Q_RANK]; c += MLA_Q_RANK
    kv_lat = z[:, c:c + MLA_KV_RANK]; c += MLA_KV_RANK
    dq = z[:, c:c + ndq]; c += ndq
    dk = z[:, c:c + ndk]; c += ndk
    dv = z[:, c:c + ndv]; c += ndv
    k_pe = z[:, c:c + MLA_ROPE]
    q = _mm(_rmsnorm(q_lat, q_norm), w_uq.reshape(MLA_Q_RANK, -1).astype(BF16), tm=640, tn=512)
    q = q.reshape(n, nh, MLA_NOPE + MLA_ROPE)
    q_pe = _rope(q[..., MLA_NOPE:], pos)
    q_abs = _bmm(jnp.transpose(q[..., :MLA_NOPE], (1, 0, 2)),
                 jnp.transpose(w_uk, (1, 2, 0)).astype(BF16), tm=640)
    q_abs = jnp.transpose(q_abs, (1, 0, 2))
    c_kv = _rmsnorm(kv_lat, kv_norm)
    k_pe = _rope(k_pe, pos)
    dq = _rope(dq.reshape(n, DIFF_KV_HEADS, DIFF_GROUP, 2, DIFF_HEAD_DIM), pos)
    dk = _rope(dk.reshape(n, DIFF_KV_HEADS, 2, DIFF_HEAD_DIM), pos)
    dv = dv.reshape(n, DIFF_KV_HEADS, DIFF_V_DIM)
    lam_init = 0.8 - 0.6 * math.exp(-0.3 * layer)
    lam = jnp.exp(jnp.sum(lq1 * lk1)) - jnp.exp(jnp.sum(lq2 * lk2)) + lam_init

    pr = lambda a: a[:np_rows].reshape((bsz, t) + a.shape[1:])
    y_mla_p = _mla_prefill(pr(q_abs), pr(q_pe), pr(c_kv), pr(k_pe), w_uv)
    y_diff_p = _diff_prefill(pr(dq), pr(dk), pr(dv), lam, subln, lam_init)

    sm = lambda a: a[np_rows:]
    mla_scale = (MLA_NOPE + MLA_ROPE) ** -0.5
    s0 = (jnp.einsum('bhr,br->bh', sm(q_abs), sm(c_kv)) + jnp.einsum('bhd,bd->bh', sm(q_pe), sm(k_pe))) * mla_scale
    rpad = DECODE_ROWS - nh
    padr = lambda a: jnp.pad(a, ((0, 0), (0, rpad)) + ((0, 0),) * (a.ndim - 2))
    o_mla = _paged_call(functools.partial(_mla_decode_step, scale=mla_scale), "mla_decode", page_table,
                        [padr(sm(q_abs)).astype(BF16), padr(sm(q_pe)).astype(BF16), padr(s0)[..., None],
                         sm(c_kv)[:, None, :]],
                        [lat_pool, pet_pool], DECODE_ROWS, MLA_KV_RANK)[:, :nh]
    y_mla_s = _bmm(jnp.transpose(o_mla, (1, 0, 2)), jnp.transpose(w_uv, (1, 0, 2)).astype(BF16), tm=ns)
    y_mla_s = jnp.transpose(y_mla_s, (1, 0, 2)).reshape(ns, nh * MLA_V)

    diff_scale = DIFF_HEAD_DIM ** -0.5
    dq_s, dk_s, dv_s = sm(dq), sm(dk), sm(dv)
    qrow = jnp.transpose(dq_s, (0, 1, 3, 2, 4))
    sel = jnp.eye(DIFF_KV_HEADS * 2, dtype=F32).reshape(DIFF_KV_HEADS, 2, 1, DIFF_KV_HEADS * 2, 1)
    qblk = (qrow[:, :, :, :, None, :] * sel[None]).reshape(ns, DECODE_ROWS, DIFF_KV_HEADS * 2 * DIFF_HEAD_DIM)
    s0d = jnp.einsum('brk,bk->br', qblk, dk_s.reshape(ns, -1)) * diff_scale
    o_d = _paged_call(functools.partial(_diff_decode_step, page=page, scale=diff_scale), "diff_decode",
                      page_table, [qblk.astype(BF16), s0d[..., None], dv_s.reshape(ns, 1, -1)],
                      [dkt_pool, dv_pool], DECODE_ROWS, DIFF_KV_HEADS * DIFF_V_DIM)
    o_d = o_d.reshape(ns, DIFF_KV_HEADS, 2, DIFF_GROUP, DIFF_KV_HEADS, DIFF_V_DIM)
    o_d = jnp.stack([o_d[:, h, :, :, h] for h in range(DIFF_KV_HEADS)], 1)
    y_diff_s = _diff_finalize(o_d[:, :, 0], o_d[:, :, 1], lam, subln, lam_init).reshape(ns, -1)

    y_mla = jnp.concatenate([y_mla_p.reshape(np_rows, -1), y_mla_s], 0)
    y_diff = jnp.concatenate([y_diff_p.reshape(np_rows, -1), y_diff_s], 0)
    outs = (pr(c_kv), pr(k_pe), sm(c_kv)[:, None], sm(k_pe)[:, None],
            pr(dk), pr(dv), sm(dk)[:, None], sm(dv)[:, None])
    return y_mla, y_diff, outs


def kernel(x_prompt, x_sample, state_s5_re, state_s5_im, cache_mla_latent, cache_mla_krope, cache_diff_k, cache_diff_v, page_table, p_prompt, p_sample, ab_w_in, s5_lambda_re, s5_lambda_im, s5_b_re, s5_b_im, s5_c_re, s5_c_im, s5_d, s5_log_dt, s5_w_glu, s5_b_glu, sg_ln_g, sg_ln_b, sg_w_s, sg_b_s, ab_w_out, cd_w_in, mla_q_norm, mla_kv_norm, mla_w_uq, mla_w_uk, mla_w_uv, diff_lq1, diff_lk1, diff_lq2, diff_lk2, diff_subln, cd_w_out, ln1_g, ln1_b, ln2_g, ln2_b, router_w, router_b, ex_w_gate, ex_b_gate, ex_w_up, ex_b_up, ex_w_down, ex_b_down, ple_w_proj, ple_w_gate):
    bsz, t, d = x_prompt.shape
    nb, ts, _ = x_sample.shape
    assert ts == 1
    np_rows = bsz * t
    npool, page = cache_mla_latent.shape[1:3]
    past_len = page_table.shape[1] * page
    pos = jnp.concatenate([jnp.tile(jnp.arange(t, dtype=jnp.int32), bsz),
                           jnp.full((nb,), past_len, jnp.int32)])
    x = jnp.concatenate([x_prompt.reshape(np_rows, d), x_sample.reshape(nb, d)], 0)
    p_all = jnp.concatenate([p_prompt.reshape(DEPTH, np_rows, -1), p_sample.reshape(DEPTH, nb, -1)], 1)
    ab_outs, cd_outs = [], []
    for i in range(DEPTH):
        j = i // 2
        if i % 2 == 0:
            w = (ab_w_in[j], s5_lambda_re[j], s5_lambda_im[j], s5_b_re[j], s5_b_im[j], s5_c_re[j], s5_c_im[j],
                 s5_d[j], s5_log_dt[j], s5_w_glu[j], s5_b_glu[j], sg_ln_g[j], sg_ln_b[j], sg_w_s[j], sg_b_s[j])
            a, b, outs = _layer_ab(x, np_rows, bsz, t, state_s5_re[j], state_s5_im[j], w)
            ab_outs.append(outs)
            w_out = ab_w_out[j]
        else:
            w = (cd_w_in[j], mla_q_norm[j], mla_kv_norm[j], mla_w_uq[j], mla_w_uk[j], mla_w_uv[j],
                 diff_lq1[j], diff_lk1[j], diff_lq2[j], diff_lk2[j], diff_subln[j])
            caches = (cache_mla_latent[j].reshape(npool, page, -1),
                      jnp.swapaxes(cache_mla_krope[j], 1, 2),
                      jnp.transpose(cache_diff_k[j], (0, 2, 3, 4, 1)).reshape(npool, -1, page),
                      cache_diff_v[j].reshape(npool, page * DIFF_KV_HEADS, DIFF_V_DIM))
            a, b, outs = _layer_cd(x, np_rows, bsz, t, pos, i, caches, page_table, w)
            cd_outs.append(outs)
            w_out = cd_w_out[j]
        ka = a.shape[1]
        x1, logits = _mix_ln_router(x, a, b, w_out[:ka].astype(BF16), w_out[ka:].astype(BF16),
                                    ln1_g[i], ln1_b[i], router_w[i], router_b[i], tm=520)
        ysel, gates = _moe(x1, logits, i, ex_w_gate, ex_b_gate, ex_w_up, ex_b_up, ex_w_down, ex_b_down)
        x = _post(x1, ysel, gates, p_all[i], ple_w_proj[i].astype(BF16), ple_w_gate[i].astype(BF16),
                  ln2_g[i], ln2_b[i], tm=208)
    y_prompt = x[:np_rows].reshape(bsz, t, d)
    y_sample = x[np_rows:].reshape(nb, ts, d)
    st = lambda k: jnp.stack([o[k] for o in ab_outs])
    sc = lambda k: jnp.stack([o[k] for o in cd_outs])
    return (y_prompt, y_sample, st(0), st(1), st(2), st(3), st(4), st(5),
            sc(0), sc(1), sc(2), sc(3), sc(4), sc(5), sc(6), sc(7))
```

```python
import functools
import math

import jax
import jax.numpy as jnp
from jax import lax
from jax.experimental import pallas as pl
from jax.experimental.pallas import tpu as pltpu

F32 = jnp.float32
BF16 = jnp.bfloat16
I32 = jnp.int32

D_MODEL = 2048
DEPTH = 2
S5_WIDTH = 1024
S5_GROUP = 16
S5_GROUPS = 64
S5_STATE = 64
S5_CHUNK = 16
SG_HEADS = 8
SG_WIDTH = 1024
SG_CHUNK = 128
MLA_HEADS = 8
MLA_Q_RANK = 512
MLA_KV_RANK = 256
MLA_NOPE = 128
MLA_ROPE = 64
MLA_V = 128
DIFF_HEADS = 8
DIFF_KV_HEADS = 2
DIFF_GROUP = 4
DIFF_HEAD_DIM = 64
DIFF_V_DIM = 128
N_EXPERTS = 32
TOP_K = 4
SWIGLU_LIMIT = 7.0
SWIGLU_ALPHA = 1.702
PLE_DIM = 256
ROPE_THETA = 10000.0
Q_BLOCK = 128
ALPHA = (2 * DEPTH) ** 0.25
NEG_INF = -1e30

MOE_SUB = 256
MOE_NSUB = 5
MOE_TF = 256
PAGES_PER_STEP = 32
DECODE_ROWS = 16
VMEM_LIMIT = 56 * 1024 * 1024


MOE_VMEM_LIMIT = 61 * 1024 * 1024


def _cparams(sem, limit=VMEM_LIMIT):
    return pltpu.CompilerParams(dimension_semantics=sem, vmem_limit_bytes=limit)


def _row_tile(rows, target):
    best = 8
    for cand in range(8, min(rows, target) + 1, 8):
        if rows % cand == 0:
            best = cand
    return best


def _gelu(x):
    return 0.5 * x * (1.0 + jnp.tanh(0.7978845608028654 * (x + 0.044715 * (x * x * x))))


def _sigmoid(x):
    return 1.0 / (1.0 + jnp.exp(-x))


def _dot(a, b):
    return jnp.dot(a, b, preferred_element_type=F32)


def _dot_nt(a, b):
    return lax.dot_general(a, b, (((1,), (1,)), ((), ())), preferred_element_type=F32)


def _ln_rows(x, g, b, eps=1e-5):
    mu = jnp.mean(x, -1, keepdims=True)
    xc = x - mu
    var = jnp.mean(xc * xc, -1, keepdims=True)
    return xc * lax.rsqrt(var + eps) * g + b


def _mm_kernel(x_ref, w_ref, o_ref, *, gelu_from):
    acc = _dot(x_ref[...].astype(BF16), w_ref[...])
    if gelu_from is None:
        o_ref[...] = acc
    else:
        j = pl.program_id(0)

        @pl.when(j < gelu_from)
        def _():
            o_ref[...] = acc

        @pl.when(j >= gelu_from)
        def _():
            o_ref[...] = _gelu(acc)


def _mm(x, w, *, tm, tn, gelu_from=None):
    m, k = x.shape
    n = w.shape[1]
    tm = _row_tile(m, tm)
    return pl.pallas_call(
        functools.partial(_mm_kernel, gelu_from=gelu_from),
        grid=(n // tn, m // tm),
        in_specs=[pl.BlockSpec((tm, k), lambda j, i: (i, 0)),
                  pl.BlockSpec((k, tn), lambda j, i: (0, j))],
        out_specs=pl.BlockSpec((tm, tn), lambda j, i: (i, j)),
        out_shape=jax.ShapeDtypeStruct((m, n), F32),
        compiler_params=_cparams(("arbitrary", "arbitrary")),
        name="mm",
    )(x, w)


def _bmm_kernel(x_ref, w_ref, o_ref):
    o_ref[...] = _dot(x_ref[...].astype(BF16), w_ref[...])


def _bmm(x, w, *, tm):
    h, m, k = x.shape
    n = w.shape[2]
    tm = _row_tile(m, tm)
    return pl.pallas_call(
        _bmm_kernel,
        grid=(h, m // tm),
        in_specs=[pl.BlockSpec((None, tm, k), lambda a, i: (a, i, 0)),
                  pl.BlockSpec((None, k, n), lambda a, i: (a, 0, 0))],
        out_specs=pl.BlockSpec((None, tm, n), lambda a, i: (a, i, 0)),
        out_shape=jax.ShapeDtypeStruct((h, m, n), F32),
        compiler_params=_cparams(("arbitrary", "arbitrary")),
        name="bmm",
    )(x, w)


def _glu_kernel(g_ref, gcol_ref, w_ref, b_ref, o_ref):
    acc = _dot(g_ref[...].astype(BF16), w_ref[...]) + b_ref[...]
    o_ref[...] = gcol_ref[...] * _sigmoid(acc)


def _glu(g5, w, b, *, tm, tn):
    m, k = g5.shape
    n = w.shape[1]
    tm = _row_tile(m, tm)
    return pl.pallas_call(
        _glu_kernel,
        grid=(n // tn, m // tm),
        in_specs=[pl.BlockSpec((tm, k), lambda j, i: (i, 0)),
                  pl.BlockSpec((tm, tn), lambda j, i: (i, j)),
                  pl.BlockSpec((k, tn), lambda j, i: (0, j)),
                  pl.BlockSpec((1, tn), lambda j, i: (0, j))],
        out_specs=pl.BlockSpec((tm, tn), lambda j, i: (i, j)),
        out_shape=jax.ShapeDtypeStruct((m, n), F32),
        compiler_params=_cparams(("arbitrary", "arbitrary")),
        name="glu",
    )(g5, g5, w, b)


def _mix_ln_router_kernel(r_ref, a_ref, b_ref, wa_ref, wb_ref, g_ref, beta_ref,
                          wrh_ref, wrl_ref, br_ref, x_ref, lg_ref):
    mix = _dot(a_ref[...].astype(BF16), wa_ref[...]) + _dot(b_ref[...].astype(BF16), wb_ref[...])
    x = _ln_rows(ALPHA * r_ref[...] + mix, g_ref[...], beta_ref[...])
    x_ref[...] = x
    xh = x.astype(BF16)
    xl = (x - xh.astype(F32)).astype(BF16)
    lg_ref[...] = (_dot(xh, wrh_ref[...]) + _dot(xl, wrh_ref[...]) + _dot(xh, wrl_ref[...])
                   + br_ref[...])


def _mix_ln_router(r, a, b, wa, wb, g, beta, wr, br, *, tm):
    m, d = r.shape
    tm = _row_tile(m, tm)
    ka, kb = a.shape[1], b.shape[1]
    ne = wr.shape[1]
    wr_p = jnp.zeros((d, 128), F32).at[:, :ne].set(wr)
    wrh = wr_p.astype(BF16)
    wrl = (wr_p - wrh.astype(F32)).astype(BF16)
    br_p = jnp.zeros((1, 128), F32).at[0, :ne].set(br)
    row = lambda i: (i, 0)
    fix = lambda i: (0, 0)
    x, lg = pl.pallas_call(
        _mix_ln_router_kernel,
        grid=(m // tm,),
        in_specs=[pl.BlockSpec((tm, d), row), pl.BlockSpec((tm, ka), row), pl.BlockSpec((tm, kb), row),
                  pl.BlockSpec((ka, d), fix), pl.BlockSpec((kb, d), fix),
                  pl.BlockSpec((1, d), fix), pl.BlockSpec((1, d), fix),
                  pl.BlockSpec((d, 128), fix), pl.BlockSpec((d, 128), fix), pl.BlockSpec((1, 128), fix)],
        out_specs=[pl.BlockSpec((tm, d), row), pl.BlockSpec((tm, 128), row)],
        out_shape=[jax.ShapeDtypeStruct((m, d), F32), jax.ShapeDtypeStruct((m, 128), F32)],
        compiler_params=_cparams(("arbitrary",)),
        name="mix_ln_router",
    )(r, a, b, wa, wb, g.reshape(1, d), beta.reshape(1, d), wrh, wrl, br_p)
    return x, lg[:, :ne]


def _post_kernel(x1_ref, *refs):
    ys_refs = refs[:TOP_K]
    gt_ref, p_ref, wp_ref, wg_ref, g_ref, beta_ref, o_ref = refs[TOP_K:]
    y = gt_ref[:, 0:1] * ys_refs[0][...]
    for k in range(1, TOP_K):
        y = y + gt_ref[:, k:k + 1] * ys_refs[k][...]
    x = _ln_rows(ALPHA * x1_ref[...] + y, g_ref[...], beta_ref[...])
    proj = _dot(p_ref[...].astype(BF16), wp_ref[...])
    gate = _sigmoid(_dot(x.astype(BF16), wg_ref[...]))
    o_ref[...] = x + proj * gate


def _post(x1, ysel, gates, p, wp, wg, g, beta, *, tm):
    m, d = x1.shape
    tm = _row_tile(m, tm)
    nb = m // tm
    kp = p.shape[1]
    gt = jnp.zeros((m, 128), F32).at[:, :TOP_K].set(gates)
    row = lambda i: (i, 0)
    fix = lambda i: (0, 0)
    ys_specs = [pl.BlockSpec((tm, d), functools.partial(lambda i, k: (k * nb + i, 0), k=k)) for k in range(TOP_K)]
    return pl.pallas_call(
        _post_kernel,
        grid=(nb,),
        in_specs=[pl.BlockSpec((tm, d), row)] + ys_specs + [
            pl.BlockSpec((tm, 128), row), pl.BlockSpec((tm, kp), row), pl.BlockSpec((kp, d), fix),
            pl.BlockSpec((d, d), fix), pl.BlockSpec((1, d), fix), pl.BlockSpec((1, d), fix)],
        out_specs=pl.BlockSpec((tm, d), row),
        out_shape=jax.ShapeDtypeStruct((m, d), F32),
        compiler_params=_cparams(("arbitrary",)),
        name="post",
    )(x1, *([ysel] * TOP_K), gt, p, wp, wg, g.reshape(1, d), beta.reshape(1, d))


def _s5_discretize(lam_re, lam_im, b_re, b_im, log_dt):
    dt = jnp.exp(log_dt)[:, None]
    lr = jnp.minimum(lam_re, -1e-4)
    li = lam_im
    mag = jnp.exp(lr * dt)
    ab_re, ab_im = mag * jnp.cos(li * dt), mag * jnp.sin(li * dt)
    nr = ab_re - 1.0
    den = lr * lr + li * li
    zr = (nr * lr + ab_im * li) / den
    zi = (ab_im * lr - nr * li) / den
    bb_re = zr[..., None] * b_re - zi[..., None] * b_im
    bb_im = zr[..., None] * b_im + zi[..., None] * b_re
    return lr * dt, li * dt, ab_re, ab_im, bb_re, bb_im


def _s5_chunk_operators(lam_re, lam_im, b_re, b_im, c_re, c_im, log_dt):
    L = S5_CHUNK
    G, P, H = S5_GROUPS, S5_STATE, S5_GROUP
    ldr, ldi, _, _, bb_re, bb_im = _s5_discretize(lam_re, lam_im, b_re, b_im, log_dt)
    k = jnp.arange(L + 1, dtype=F32)[:, None, None]
    pk_mag = jnp.exp(k * ldr)
    pk_re, pk_im = pk_mag * jnp.cos(k * ldi), pk_mag * jnp.sin(k * ldi)
    abb_re = pk_re[..., None] * bb_re - pk_im[..., None] * bb_im
    abb_im = pk_re[..., None] * bb_im + pk_im[..., None] * bb_re
    kern = (jnp.einsum('gcp,kgpd->kgcd', c_re, abb_re[:L]) - jnp.einsum('gcp,kgpd->kgcd', c_im, abb_im[:L]))
    t = jnp.arange(L)
    lag = t[None, :] - t[:, None]
    toe = jnp.where((lag >= 0)[:, :, None, None, None], kern[jnp.clip(lag, 0, L - 1)], 0.0)
    m_op = jnp.transpose(toe, (2, 0, 4, 1, 3)).reshape(G, L * H, L * H)
    rev_re = abb_re[L - 1 - t]
    rev_im = abb_im[L - 1 - t]
    bs_op = jnp.concatenate([jnp.transpose(rev_re, (1, 0, 3, 2)).reshape(G, L * H, P),
                             jnp.transpose(rev_im, (1, 0, 3, 2)).reshape(G, L * H, P)], -1)
    ca_re = c_re[None] * pk_re[1:, :, None, :] - c_im[None] * pk_im[1:, :, None, :]
    ca_im = c_re[None] * pk_im[1:, :, None, :] + c_im[None] * pk_re[1:, :, None, :]
    cs_op = jnp.concatenate([jnp.transpose(ca_re, (1, 3, 0, 2)).reshape(G, P, L * H),
                             -jnp.transpose(ca_im, (1, 3, 0, 2)).reshape(G, P, L * H)], 1)
    return m_op, bs_op, cs_op, pk_re[L], pk_im[L]


def _s5_state_kernel(u_ref, bs_ref, s_ref, *, gpb):
    for q in range(gpb):
        s_ref[:, q * 128:(q + 1) * 128] = _dot(u_ref[q].astype(BF16), bs_ref[q])


def _s5_scan_kernel(sr_ref, si_ref, ar_ref, ai_ref, x0r_ref, x0i_ref, er_ref, ei_ref):
    nchunk = sr_ref.shape[0]
    ar, ai = ar_ref[...], ai_ref[...]

    def body(j, carry):
        xr, xi = carry
        x0r_ref[j] = xr
        x0i_ref[j] = xi
        return ar * xr - ai * xi + sr_ref[j], ar * xi + ai * xr + si_ref[j]

    zero = jnp.zeros(sr_ref.shape[1:], F32)
    xr, xi = lax.fori_loop(0, nchunk, body, (zero, zero))
    er_ref[...] = xr
    ei_ref[...] = xi


def _s5_out_kernel(u_ref, x0_ref, m_ref, cs_ref, d_ref, o_ref, *, gpb):
    for q in range(gpb):
        u = u_ref[q]
        y = (_dot(u.astype(BF16), m_ref[q]) + _dot(x0_ref[:, q * 128:(q + 1) * 128].astype(BF16), cs_ref[q])
             + d_ref[q] * u)
        o_ref[q] = _gelu(y)


def _s5_prompt(u, ops, d_skip):
    m_op, bs_op, cs_op, al_re, al_im = ops
    bsz, t, _ = u.shape
    L, G, P, H = S5_CHUNK, S5_GROUPS, S5_STATE, S5_GROUP
    nchunk = t // L
    rows = nchunk * bsz
    gpb = 8
    ug = jnp.transpose(u.reshape(bsz, nchunk, L, G, H), (3, 1, 0, 2, 4)).reshape(G, rows, L * H)
    s_loc = pl.pallas_call(
        functools.partial(_s5_state_kernel, gpb=gpb),
        grid=(G // gpb,),
        in_specs=[pl.BlockSpec((gpb, rows, L * H), lambda g: (g, 0, 0)),
                  pl.BlockSpec((gpb, L * H, 2 * P), lambda g: (g, 0, 0))],
        out_specs=pl.BlockSpec((rows, gpb * 2 * P), lambda g: (0, g)),
        out_shape=jax.ShapeDtypeStruct((rows, G * 2 * P), F32),
        compiler_params=_cparams(("arbitrary",)),
        name="s5_state",
    )(ug, bs_op.astype(BF16))
    s4 = s_loc.reshape(nchunk, bsz, G, 2, P)
    sr = s4[:, :, :, 0].reshape(nchunk, bsz, G * P)
    si = s4[:, :, :, 1].reshape(nchunk, bsz, G * P)
    lw = 512
    seq = pl.BlockSpec((nchunk, bsz, lw), lambda i: (0, 0, i))
    vec = pl.BlockSpec((1, lw), lambda i: (0, i))
    fin = pl.BlockSpec((bsz, lw), lambda i: (0, i))
    x0r, x0i, er, ei = pl.pallas_call(
        _s5_scan_kernel,
        grid=(G * P // lw,),
        in_specs=[seq, seq, vec, vec],
        out_specs=[seq, seq, fin, fin],
        out_shape=[jax.ShapeDtypeStruct((nchunk, bsz, G * P), F32)] * 2
        + [jax.ShapeDtypeStruct((bsz, G * P), F32)] * 2,
        compiler_params=_cparams(("arbitrary",)),
        name="s5_scan",
    )(sr, si, al_re.reshape(1, G * P), al_im.reshape(1, G * P))
    x0 = jnp.stack([x0r.reshape(rows, G, P), x0i.reshape(rows, G, P)], 2).reshape(rows, G * 2 * P)
    d_t = jnp.tile(d_skip, (1, L)).reshape(G, 1, L * H)
    g5 = pl.pallas_call(
        functools.partial(_s5_out_kernel, gpb=gpb),
        grid=(G // gpb,),
        in_specs=[pl.BlockSpec((gpb, rows, L * H), lambda g: (g, 0, 0)),
                  pl.BlockSpec((rows, gpb * 2 * P), lambda g: (0, g)),
                  pl.BlockSpec((gpb, L * H, L * H), lambda g: (g, 0, 0)),
                  pl.BlockSpec((gpb, 2 * P, L * H), lambda g: (g, 0, 0)),
                  pl.BlockSpec((gpb, 1, L * H), lambda g: (g, 0, 0))],
        out_specs=pl.BlockSpec((gpb, rows, L * H), lambda g: (g, 0, 0)),
        out_shape=jax.ShapeDtypeStruct((G, rows, L * H), F32),
        compiler_params=_cparams(("arbitrary",)),
        name="s5_out",
    )(ug, x0, m_op.astype(BF16), cs_op.astype(BF16), d_t)
    g5 = jnp.transpose(g5.reshape(G, nchunk, bsz, L, H), (2, 1, 3, 0, 4)).reshape(bsz, t, G * H)
    return g5, er.reshape(bsz, G, P), ei.reshape(bsz, G, P)


def _s5_step_kernel(u_ref, x0r_ref, x0i_ref, ar_ref, ai_ref, bbr_ref, bbi_ref, cr_ref, ci_ref, d_ref,
                    g_ref, xr_ref, xi_ref, *, nblk, bw, sw):
    for q in range(nblk):
        u = u_ref[:, q * bw:(q + 1) * bw]
        ub = u.astype(BF16)
        sl = slice(q * sw, (q + 1) * sw)
        ar, ai = ar_ref[:, sl], ai_ref[:, sl]
        x0r, x0i = x0r_ref[:, sl], x0i_ref[:, sl]
        xr = ar * x0r - ai * x0i + _dot(ub, bbr_ref[q])
        xi = ar * x0i + ai * x0r + _dot(ub, bbi_ref[q])
        xr_ref[:, sl] = xr
        xi_ref[:, sl] = xi
        y = _dot(xr.astype(BF16), cr_ref[q]) - _dot(xi.astype(BF16), ci_ref[q]) + d_ref[:, q * bw:(q + 1) * bw] * u
        g_ref[:, q * bw:(q + 1) * bw] = _gelu(y)


def _s5_step(u, x0r, x0i, lam_re, lam_im, b_re, b_im, c_re, c_im, log_dt, d_skip):
    n = u.shape[0]
    G, P, H = S5_GROUPS, S5_STATE, S5_GROUP
    gb = 16
    nblk = G // gb
    _, _, ab_re, ab_im, bb_re, bb_im = _s5_discretize(lam_re, lam_im, b_re, b_im, log_dt)
    eye = jnp.eye(gb, dtype=F32)

    def in_blocks(w):
        w4 = w.reshape(nblk, gb, P, H)
        return jnp.einsum('qgph,gk->qghkp', w4, eye).reshape(nblk, gb * H, gb * P)

    def out_blocks(w):
        w4 = w.reshape(nblk, gb, H, P)
        return jnp.einsum('qghp,gk->qgpkh', w4, eye).reshape(nblk, gb * P, gb * H)

    full = lambda i: (0, 0)
    full3 = lambda i: (0, 0, 0)
    g5, xr, xi = pl.pallas_call(
        functools.partial(_s5_step_kernel, nblk=nblk, bw=gb * H, sw=gb * P),
        grid=(1,),
        in_specs=[pl.BlockSpec((n, G * H), full), pl.BlockSpec((n, G * P), full), pl.BlockSpec((n, G * P), full),
                  pl.BlockSpec((1, G * P), full), pl.BlockSpec((1, G * P), full),
                  pl.BlockSpec((nblk, gb * H, gb * P), full3), pl.BlockSpec((nblk, gb * H, gb * P), full3),
                  pl.BlockSpec((nblk, gb * P, gb * H), full3), pl.BlockSpec((nblk, gb * P, gb * H), full3),
                  pl.BlockSpec((1, G * H), full)],
        out_specs=[pl.BlockSpec((n, G * H), full), pl.BlockSpec((n, G * P), full), pl.BlockSpec((n, G * P), full)],
        out_shape=[jax.ShapeDtypeStruct((n, G * H), F32), jax.ShapeDtypeStruct((n, G * P), F32),
                   jax.ShapeDtypeStruct((n, G * P), F32)],
        compiler_params=_cparams(("arbitrary",)),
        name="s5_step",
    )(u, x0r.reshape(n, G * P), x0i.reshape(n, G * P), ab_re.reshape(1, G * P), ab_im.reshape(1, G * P),
      in_blocks(bb_re).astype(BF16), in_blocks(bb_im).astype(BF16),
      out_blocks(c_re).astype(BF16), out_blocks(c_im).astype(BF16), d_skip.reshape(1, G * H))
    return g5, xr.reshape(n, G, P), xi.reshape(n, G, P)


def _sg_kernel(u_ref, v_ref, g_ref, b_ref, w_ref, bias_ref, o_ref, vl_ref, *, cpb):
    i = pl.program_id(0)
    v = _ln_rows(v_ref[...], g_ref[...], b_ref[...])
    vb = v.astype(BF16)
    hd = SG_WIDTH // SG_HEADS
    for h in range(SG_HEADS):
        sl = slice(h * hd, (h + 1) * hd)
        s = _dot(w_ref[h], vb[:, sl]) + bias_ref[:, sl]
        o_ref[:, sl] = u_ref[:, sl] * s

    @pl.when(i % cpb == cpb - 1)
    def _():
        vl_ref[...] = v


def _spatial_gating_prompt(z, col0, bsz, t, ln_g, ln_b, w_s, b_s):
    cpb = t // SG_CHUNK
    cu = col0 // SG_WIDTH
    w = (w_s * jnp.tril(jnp.ones((SG_CHUNK, SG_CHUNK), F32))).astype(BF16)
    bias = jnp.repeat(b_s.T, SG_WIDTH // SG_HEADS, axis=1)
    fix = lambda i: (0, 0)
    return pl.pallas_call(
        functools.partial(_sg_kernel, cpb=cpb),
        grid=(bsz * cpb,),
        in_specs=[pl.BlockSpec((SG_CHUNK, SG_WIDTH), lambda i: (i, cu)),
                  pl.BlockSpec((SG_CHUNK, SG_WIDTH), lambda i: (i, cu + 1)),
                  pl.BlockSpec((1, SG_WIDTH), fix), pl.BlockSpec((1, SG_WIDTH), fix),
                  pl.BlockSpec((SG_HEADS, SG_CHUNK, SG_CHUNK), lambda i: (0, 0, 0)),
                  pl.BlockSpec((SG_CHUNK, SG_WIDTH), fix)],
        out_specs=[pl.BlockSpec((SG_CHUNK, SG_WIDTH), lambda i: (i, 0)),
                   pl.BlockSpec((None, SG_CHUNK, SG_WIDTH), lambda i: (i // cpb, 0, 0))],
        out_shape=[jax.ShapeDtypeStruct((bsz * t, SG_WIDTH), F32),
                   jax.ShapeDtypeStruct((bsz, SG_CHUNK, SG_WIDTH), F32)],
        compiler_params=_cparams(("arbitrary",)),
        name="spatial_gating",
    )(z, z, ln_g.reshape(1, -1), ln_b.reshape(1, -1), w, bias)


def _sg_step_kernel(u_ref, v_ref, g_ref, b_ref, w0_ref, b0_ref, o_ref, vn_ref):
    v = _ln_rows(v_ref[...], g_ref[...], b_ref[...])
    vn_ref[...] = v
    o_ref[...] = u_ref[...] * (w0_ref[...] * v + b0_ref[...])


def _spatial_gating_step(zs, ln_g, ln_b, w_s, b_s):
    n = zs.shape[0]
    hd = SG_WIDTH // SG_HEADS
    w0 = jnp.repeat(w_s[:, 0, 0], hd).reshape(1, SG_WIDTH)
    b0 = jnp.repeat(b_s[:, 0], hd).reshape(1, SG_WIDTH)
    fix = lambda i: (0, 0)
    return pl.pallas_call(
        _sg_step_kernel,
        grid=(1,),
        in_specs=[pl.BlockSpec((n, SG_WIDTH), lambda i: (0, 0)), pl.BlockSpec((n, SG_WIDTH), lambda i: (0, 1)),
                  pl.BlockSpec((1, SG_WIDTH), fix), pl.BlockSpec((1, SG_WIDTH), fix),
                  pl.BlockSpec((1, SG_WIDTH), fix), pl.BlockSpec((1, SG_WIDTH), fix)],
        out_specs=[pl.BlockSpec((n, SG_WIDTH), fix), pl.BlockSpec((n, SG_WIDTH), fix)],
        out_shape=[jax.ShapeDtypeStruct((n, SG_WIDTH), F32)] * 2,
        compiler_params=_cparams(("arbitrary",)),
        name="spatial_gating_step",
    )(zs, zs, ln_g.reshape(1, -1), ln_b.reshape(1, -1), w0, b0)


def _online_softmax_update(s, m_sc, l_sc):
    m_old = m_sc[...]
    m_new = jnp.maximum(m_old, jnp.max(s, -1, keepdims=True))
    a = jnp.exp(m_old - m_new)
    p = jnp.exp(s - m_new)
    l_sc[...] = a * l_sc[...] + jnp.sum(p, -1, keepdims=True)
    m_sc[...] = m_new
    return a, p


def _causal_mask(s, qi, kb, tk):
    row = lax.broadcasted_iota(jnp.int32, s.shape, 0)
    col = lax.broadcasted_iota(jnp.int32, s.shape, 1)
    qpos = qi * Q_BLOCK + (row & (Q_BLOCK - 1))
    return jnp.where(qpos >= kb * tk + col, s, NEG_INF)


def _key_block_cases(qi, kb, tk, step):
    needed = kb * tk <= qi * Q_BLOCK + Q_BLOCK - 1
    crosses = (kb + 1) * tk - 1 > qi * Q_BLOCK

    @pl.when(needed & crosses)
    def _():
        step(True)

    @pl.when(needed & jnp.logical_not(crosses))
    def _():
        step(False)


def _mla_prefill_kernel(ql_ref, qp_ref, ckv_ref, kpe_ref, wuv_ref, y_ref, m_sc, l_sc, acc_sc, *, tk):
    qi, kb = pl.program_id(1), pl.program_id(2)

    @pl.when(kb == 0)
    def _():
        m_sc[...] = jnp.full(m_sc.shape, NEG_INF, F32)
        l_sc[...] = jnp.zeros(l_sc.shape, F32)
        acc_sc[...] = jnp.zeros(acc_sc.shape, F32)

    def step(masked):
        k1 = ckv_ref[...].astype(BF16)
        s = _dot_nt(ql_ref[...], k1) + _dot_nt(qp_ref[...], kpe_ref[...].astype(BF16))
        if masked:
            s = _causal_mask(s, qi, kb, tk)
        a, p = _online_softmax_update(s, m_sc, l_sc)
        acc_sc[...] = a * acc_sc[...] + _dot(p.astype(BF16), k1)

    _key_block_cases(qi, kb, tk, step)

    @pl.when(kb == pl.num_programs(2) - 1)
    def _():
        o = (acc_sc[...] / l_sc[...]).astype(BF16)
        for h in range(MLA_HEADS):
            y_ref[:, h * MLA_V:(h + 1) * MLA_V] = _dot(o[h * Q_BLOCK:(h + 1) * Q_BLOCK],
                                                        wuv_ref[:, h * MLA_V:(h + 1) * MLA_V])


def _mla_prefill(q_lat, q_pe, c_kv, k_pe, w_uv, *, tk=512):
    bsz, t, nh, r = q_lat.shape
    nq = t // Q_BLOCK
    tk = min(tk, t)
    rows = nh * Q_BLOCK

    scale = (MLA_NOPE + MLA_ROPE) ** -0.5

    def blocks(q):
        d = q.shape[-1]
        q = jnp.transpose((q * scale).reshape(bsz, nq, Q_BLOCK, nh, d), (0, 1, 3, 2, 4))
        return q.reshape(bsz, nq, rows, d).astype(BF16)

    kmap = lambda b, qi, kb: (b, jnp.minimum(kb, (qi * Q_BLOCK + Q_BLOCK - 1) // tk), 0)
    return pl.pallas_call(
        functools.partial(_mla_prefill_kernel, tk=tk),
        grid=(bsz, nq, t // tk),
        in_specs=[pl.BlockSpec((None, None, rows, r), lambda b, qi, kb: (b, qi, 0, 0)),
                  pl.BlockSpec((None, None, rows, MLA_ROPE), lambda b, qi, kb: (b, qi, 0, 0)),
                  pl.BlockSpec((None, tk, r), kmap),
                  pl.BlockSpec((None, tk, MLA_ROPE), kmap),
                  pl.BlockSpec((r, nh * MLA_V), lambda b, qi, kb: (0, 0))],
        out_specs=pl.BlockSpec((None, Q_BLOCK, nh * MLA_V), lambda b, qi, kb: (b, qi, 0)),
        out_shape=jax.ShapeDtypeStruct((bsz, t, nh * MLA_V), F32),
        scratch_shapes=[pltpu.VMEM((rows, 1), F32), pltpu.VMEM((rows, 1), F32), pltpu.VMEM((rows, r), F32)],
        compiler_params=_cparams(("arbitrary", "arbitrary", "arbitrary")),
        name="mla_prefill",
    )(blocks(q_lat), blocks(q_pe), c_kv, k_pe, w_uv.reshape(r, nh * MLA_V).astype(BF16))


def _diff_finalize(o0, o1, lam, subln, lam_init):
    o = o0 - lam * o1
    ms = jnp.mean(o * o, -1, keepdims=True)
    return o * lax.rsqrt(ms + 1e-5) * subln * (1.0 - lam_init)


def _diff_prefill_kernel(q_ref, k_ref, v_ref, lam_ref, sub_ref, y_ref, m_sc, l_sc, acc_sc, *, tk, lam_init):
    qi, kb = pl.program_id(1), pl.program_id(2)

    @pl.when(kb == 0)
    def _():
        m_sc[...] = jnp.full(m_sc.shape, NEG_INF, F32)
        l_sc[...] = jnp.zeros(l_sc.shape, F32)
        acc_sc[...] = jnp.zeros(acc_sc.shape, F32)

    def step(masked):
        vb = v_ref[...].astype(BF16)
        for mi in range(2):
            s = _dot_nt(q_ref[mi], k_ref[mi].astype(BF16))
            if masked:
                s = _causal_mask(s, qi, kb, tk)
            a, p = _online_softmax_update(s, m_sc.at[mi], l_sc.at[mi])
            acc_sc[mi] = a * acc_sc[mi] + _dot(p.astype(BF16), vb)

    _key_block_cases(qi, kb, tk, step)

    @pl.when(kb == pl.num_programs(2) - 1)
    def _():
        o = _diff_finalize(acc_sc[0] / l_sc[0], acc_sc[1] / l_sc[1], lam_ref[...], sub_ref[...], lam_init)
        for g in range(DIFF_GROUP):
            y_ref[:, g * DIFF_V_DIM:(g + 1) * DIFF_V_DIM] = o[g * Q_BLOCK:(g + 1) * Q_BLOCK]


def _diff_prefill(dq, dk, dv, lam, subln, lam_init, *, tk=512):
    bsz, t = dq.shape[:2]
    nq = t // Q_BLOCK
    tk = min(tk, t)
    rows = DIFF_GROUP * Q_BLOCK
    d = DIFF_HEAD_DIM
    q = (dq * d ** -0.5).reshape(bsz, nq, Q_BLOCK, DIFF_KV_HEADS, DIFF_GROUP, 2, d)
    q = jnp.transpose(q, (0, 3, 1, 5, 4, 2, 6))
    q = q.reshape(bsz * DIFF_KV_HEADS, nq, 2, rows, d).astype(BF16)
    k = jnp.transpose(dk, (0, 2, 3, 1, 4)).reshape(bsz * DIFF_KV_HEADS, 2, t, d)
    v = jnp.transpose(dv, (0, 2, 1, 3)).reshape(bsz * DIFF_KV_HEADS, t, DIFF_V_DIM)
    last = lambda qi: (qi * Q_BLOCK + Q_BLOCK - 1) // tk
    width = DIFF_GROUP * DIFF_V_DIM
    return pl.pallas_call(
        functools.partial(_diff_prefill_kernel, tk=tk, lam_init=lam_init),
        grid=(bsz * DIFF_KV_HEADS, nq, t // tk),
        in_specs=[pl.BlockSpec((None, None, 2, rows, d), lambda a, qi, kb: (a, qi, 0, 0, 0)),
                  pl.BlockSpec((None, 2, tk, d), lambda a, qi, kb: (a, 0, jnp.minimum(kb, last(qi)), 0)),
                  pl.BlockSpec((None, tk, DIFF_V_DIM), lambda a, qi, kb: (a, jnp.minimum(kb, last(qi)), 0)),
                  pl.BlockSpec((1, DIFF_V_DIM), lambda a, qi, kb: (0, 0)),
                  pl.BlockSpec((1, DIFF_V_DIM), lambda a, qi, kb: (0, 0))],
        out_specs=pl.BlockSpec((None, Q_BLOCK, width),
                               lambda a, qi, kb: (a // DIFF_KV_HEADS, qi, a % DIFF_KV_HEADS)),
        out_shape=jax.ShapeDtypeStruct((bsz, t, DIFF_KV_HEADS * width), F32),
        scratch_shapes=[pltpu.VMEM((2, rows, 1), F32), pltpu.VMEM((2, rows, 1), F32),
                        pltpu.VMEM((2, rows, DIFF_V_DIM), F32)],
        compiler_params=_cparams(("arbitrary", "arbitrary", "arbitrary")),
        name="diff_prefill",
    )(q, k, v, jnp.full((1, DIFF_V_DIM), lam, F32), subln.reshape(1, DIFF_V_DIM))


def _decode_init(s0_ref, v0_ref, m_sc, l_sc, acc_sc):
    m_sc[...] = s0_ref[...]
    l_sc[...] = jnp.ones(l_sc.shape, F32)
    acc_sc[...] = jnp.broadcast_to(v0_ref[...], acc_sc.shape)


def _mla_decode_step(small, pages, m_sc, l_sc, acc_sc, *, npg, scale):
    q1_ref, q2_ref = small[0], small[1]
    lat_ref, pet_ref = pages
    q1, q2 = q1_ref[...], q2_ref[...]
    lats, scores = [], []
    for i in range(npg):
        lat = lat_ref[i].astype(BF16)
        scores.append(_dot_nt(q1, lat) + _dot(q2, pet_ref[i].astype(BF16)))
        lats.append(lat)
    s = jnp.concatenate(scores, axis=1) * scale
    a, p = _online_softmax_update(s, m_sc, l_sc)
    pb = p.astype(BF16)
    pw = scores[0].shape[1]
    acc = a * acc_sc[...]
    for i in range(npg):
        acc = acc + _dot(pb[:, i * pw:(i + 1) * pw], lats[i])
    acc_sc[...] = acc


def _diff_decode_step(small, pages, m_sc, l_sc, acc_sc, *, npg, page, scale):
    q = small[0][...]
    kt_ref, v_ref = pages
    scores = [_dot(q, kt_ref[i].astype(BF16)) for i in range(npg)]
    s = jnp.concatenate(scores, axis=1) * scale
    a, p = _online_softmax_update(s, m_sc, l_sc)
    pb = p.astype(BF16)
    for h in range(DIFF_KV_HEADS):
        sl = slice(h * DIFF_V_DIM, (h + 1) * DIFF_V_DIM)
        acc = a * acc_sc[:, sl]
        for i in range(npg):
            vh = v_ref[i, pl.ds(h, page, stride=DIFF_KV_HEADS), :].astype(BF16)
            acc = acc + _dot(pb[:, i * page:(i + 1) * page], vh)
        acc_sc[:, sl] = acc


def _paged_kernel(pt_ref, *refs, n_small, n_pools, npg, step_fn):
    small = refs[:n_small]
    pools = refs[n_small:n_small + n_pools]
    o_ref = refs[n_small + n_pools]
    bufs = refs[n_small + n_pools + 1:n_small + 2 * n_pools + 1]
    sem, m_sc, l_sc, acc_sc = refs[n_small + 2 * n_pools + 1:]
    b, sg = pl.program_id(0), pl.program_id(1)
    nb, ng = pl.num_programs(0), pl.num_programs(1)

    def page_copy(k, i, slot, pid):
        return pltpu.make_async_copy(pools[k].at[pid], bufs[k].at[slot, i], sem.at[k, slot])

    def start_group(bb, grp, slot):
        for k in range(n_pools):
            for i in range(npg):
                page_copy(k, i, slot, pt_ref[bb, grp * npg + i]).start()

    def wait_group(slot):
        for k in range(n_pools):
            for i in range(npg):
                page_copy(k, i, slot, 0).wait()

    @pl.when((b == 0) & (sg == 0))
    def _():
        start_group(0, 0, 0)

    @pl.when(sg == 0)
    def _():
        _decode_init(small[-2], small[-1], m_sc, l_sc, acc_sc)

    start_group(b, 2 * sg + 1, 1)
    wait_group(0)
    step_fn(small, [buf.at[0] for buf in bufs], m_sc, l_sc, acc_sc)

    row_done = sg == ng - 1

    @pl.when(jnp.logical_not(row_done & (b == nb - 1)))
    def _():
        start_group(jnp.where(row_done, b + 1, b), jnp.where(row_done, 0, 2 * sg + 2), 0)

    wait_group(1)
    step_fn(small, [buf.at[1] for buf in bufs], m_sc, l_sc, acc_sc)

    @pl.when(row_done)
    def _():
        o_ref[...] = acc_sc[...] / l_sc[...]


def _paged_call(step_fn, name, page_table, small, pools, r, dv):
    bsz, npages = page_table.shape
    assert npages % 2 == 0
    npg = math.gcd(PAGES_PER_STEP, npages // 2)
    per_b = lambda b, s, pt: (b, 0, 0)
    in_specs = [pl.BlockSpec((None,) + a.shape[1:], per_b) for a in small]
    in_specs += [pl.BlockSpec(memory_space=pl.ANY) for _ in pools]
    grid_spec = pltpu.PrefetchScalarGridSpec(
        num_scalar_prefetch=1,
        grid=(bsz, npages // (2 * npg)),
        in_specs=in_specs,
        out_specs=pl.BlockSpec((None, r, dv), per_b),
        scratch_shapes=[pltpu.VMEM((2, npg) + pool.shape[1:], pool.dtype) for pool in pools]
        + [pltpu.SemaphoreType.DMA((len(pools), 2)),
           pltpu.VMEM((r, 1), F32), pltpu.VMEM((r, 1), F32), pltpu.VMEM((r, dv), F32)],
    )
    return pl.pallas_call(
        functools.partial(_paged_kernel, n_small=len(small), n_pools=len(pools), npg=npg,
                          step_fn=functools.partial(step_fn, npg=npg)),
        grid_spec=grid_spec,
        out_shape=jax.ShapeDtypeStruct((bsz, r, dv), F32),
        compiler_params=_cparams(("arbitrary", "arbitrary")),
        name=name,
    )(page_table, *small, *pools)


def _moe_kernel(first_ref, e_ref, nsub_ref, oblk_ref, *refs):
    del first_ref, e_ref, oblk_ref
    x_refs = refs[:MOE_NSUB]
    wg_ref, bg_ref, wu_ref, bu_ref, wd_ref, bd_ref, o_ref, wg_sc, wu_sc, wd_sc = refs[MOE_NSUB:]
    s, j = pl.program_id(0), pl.program_id(1)
    ns = nsub_ref[s]

    @pl.when((ns == 0) & (j == 0))
    def _():
        o_ref[...] = jnp.zeros(o_ref.shape, F32)

    @pl.when(ns > 0)
    def _():
        @pl.when(j == 0)
        def _():
            o_ref[...] = jnp.broadcast_to(bd_ref[...], o_ref.shape)

        for cnt in range(1, MOE_NSUB + 1):
            @pl.when(ns == cnt)
            def _(cnt=cnt):
                wg_sc[...] = wg_ref[...].astype(BF16)
                wu_sc[...] = wu_ref[...].astype(BF16)
                wd_sc[...] = wd_ref[...].astype(BF16)
                for i in range(cnt):
                    xb = x_refs[i][...].astype(BF16)
                    g = jnp.minimum(_dot(xb, wg_sc[...]) + bg_ref[...], SWIGLU_LIMIT)
                    u = jnp.clip(_dot(xb, wu_sc[...]) + bu_ref[...], -SWIGLU_LIMIT, SWIGLU_LIMIT)
                    h = (u + 1.0) * (g * _sigmoid(SWIGLU_ALPHA * g))
                    o_ref[i * MOE_SUB:(i + 1) * MOE_SUB, :] += _dot(h.astype(BF16), wd_sc[...])


def _moe(x, logits, layer, wg, bg, wu, bu, wd, bd):
    n, d = x.shape
    f = wg.shape[-1]
    sub, nsub_max, tf = MOE_SUB, MOE_NSUB, MOE_TF
    tm = sub * nsub_max
    nf = f // tf
    nk = n * TOP_K
    nblk = nk // sub + N_EXPERTS
    nsb = (nblk + (nsub_max - 1) * N_EXPERTS) // nsub_max
    top_v, top_i = lax.top_k(logits, TOP_K)
    gates = jax.nn.softmax(top_v, axis=-1)
    flat_e = top_i.reshape(-1).astype(I32)
    order = jnp.argsort(flat_e).astype(I32)
    inv = jnp.argsort(order).astype(I32)
    eids = jnp.arange(N_EXPERTS, dtype=I32)
    counts = jnp.sum((flat_e[:, None] == eids[None, :]).astype(I32), axis=0)
    grp_start = (jnp.cumsum(counts) - counts).astype(I32)
    blocks_e = (counts + sub - 1) // sub
    blk_end = jnp.cumsum(blocks_e).astype(I32)
    blk_start = blk_end - blocks_e
    sb_per_e = (blocks_e + nsub_max - 1) // nsub_max
    sb_end = jnp.cumsum(sb_per_e).astype(I32)
    sb_start = sb_end - sb_per_e
    r = jnp.arange(nblk * sub, dtype=I32)
    e_r = jnp.minimum(jnp.sum((r[:, None] // sub >= blk_end[None, :]).astype(I32), axis=1), N_EXPERTS - 1)
    rank_r = r - blk_start[e_r] * sub
    valid_r = (r // sub < blk_end[-1]) & (rank_r < counts[e_r])
    tok = jnp.where(valid_r, order[jnp.clip(grp_start[e_r] + rank_r, 0, nk - 1)] // TOP_K, 0)
    xs = x[tok]
    n_used = sb_end[-1]
    sb_ids = jnp.arange(nsb, dtype=I32)
    sbc = jnp.minimum(sb_ids, n_used - 1)
    sb_e = jnp.minimum(jnp.sum((sbc[:, None] >= sb_end[None, :]).astype(I32), axis=1), N_EXPERTS - 1)
    kk = sbc - sb_start[sb_e]
    first = blk_start[sb_e] + kk * nsub_max
    nsub = jnp.where(sb_ids < n_used, jnp.clip(blocks_e[sb_e] - kk * nsub_max, 0, nsub_max), 0).astype(I32)

    jj = lambda s, j, ns: jnp.where(ns[s] > 0, j, nf - 1)

    def x_spec(i):
        return pl.BlockSpec((sub, d), lambda s, j, fr, e, ns, ob: (jnp.minimum(fr[s] + i, nblk - 1), 0))

    grid_spec = pltpu.PrefetchScalarGridSpec(
        num_scalar_prefetch=4,
        grid=(nsb, nf),
        in_specs=[x_spec(i) for i in range(nsub_max)] + [
            pl.BlockSpec((None, None, d, tf), lambda s, j, fr, e, ns, ob: (layer, e[s], 0, jj(s, j, ns))),
            pl.BlockSpec((None, None, 1, tf), lambda s, j, fr, e, ns, ob: (layer, e[s], 0, jj(s, j, ns))),
            pl.BlockSpec((None, None, d, tf), lambda s, j, fr, e, ns, ob: (layer, e[s], 0, jj(s, j, ns))),
            pl.BlockSpec((None, None, 1, tf), lambda s, j, fr, e, ns, ob: (layer, e[s], 0, jj(s, j, ns))),
            pl.BlockSpec((None, None, tf, d), lambda s, j, fr, e, ns, ob: (layer, e[s], jj(s, j, ns), 0)),
            pl.BlockSpec((None, None, 1, d), lambda s, j, fr, e, ns, ob: (layer, e[s], 0, 0))],
        out_specs=pl.BlockSpec((tm, d), lambda s, j, fr, e, ns, ob: (ob[s], 0)),
        scratch_shapes=[pltpu.VMEM((d, tf), BF16), pltpu.VMEM((d, tf), BF16), pltpu.VMEM((tf, d), BF16)],
    )
    nl, ne = wg.shape[:2]
    yb = pl.pallas_call(
        _moe_kernel,
        grid_spec=grid_spec,
        out_shape=jax.ShapeDtypeStruct((nsb * tm, d), F32),
        compiler_params=_cparams(("arbitrary", "arbitrary"), MOE_VMEM_LIMIT),
        name="moe_experts",
    )(first.astype(I32), sb_e.astype(I32), nsub, sb_ids, *([xs] * nsub_max),
      wg, bg.reshape(nl, ne, 1, f), wu, bu.reshape(nl, ne, 1, f), wd, bd.reshape(nl, ne, 1, d))
    rank_a = inv - grp_start[flat_e]
    dest = (sb_start[flat_e] + rank_a // tm) * tm + rank_a % tm
    return yb[dest.reshape(n, TOP_K).T.reshape(-1)], gates


def _rmsnorm(x, g, eps=1e-6):
    return x * lax.rsqrt(jnp.mean(x * x, -1, keepdims=True) + eps) * g


def _rope(x, pos):
    d = x.shape[-1]
    inv = ROPE_THETA ** (-jnp.arange(0, d, 2, dtype=F32) / d)
    ang = pos.astype(F32)[:, None] * inv[None, :]
    ang = ang.reshape((ang.shape[0],) + (1,) * (x.ndim - 2) + (ang.shape[1],))
    cos, sin = jnp.cos(ang), jnp.sin(ang)
    x1, x2 = jnp.split(x, 2, axis=-1)
    return jnp.concatenate([x1 * cos - x2 * sin, x1 * sin + x2 * cos], -1)


def _layer_ab(x, np_rows, bsz, t, state_re, state_im, w):
    (w_in, lam_re, lam_im, b_re, b_im, c_re, c_im, d_skip, log_dt, w_glu, b_glu,
     ln_g, ln_b, w_s, b_s) = w
    z = _mm(x, w_in.astype(BF16), tm=640, tn=S5_WIDTH, gelu_from=1)
    ops = _s5_chunk_operators(lam_re, lam_im, b_re, b_im, c_re, c_im, log_dt)
    g5_p, pr, pi = _s5_prompt(z[:np_rows, :S5_WIDTH].reshape(bsz, t, S5_WIDTH), ops, d_skip)
    g5_s, sr, si = _s5_step(z[np_rows:, :S5_WIDTH], state_re, state_im,
                            lam_re, lam_im, b_re, b_im, c_re, c_im, log_dt, d_skip)
    g5 = jnp.concatenate([g5_p.reshape(np_rows, S5_WIDTH), g5_s], 0)
    o5 = _glu(g5, w_glu.astype(BF16), b_glu.reshape(1, -1), tm=640, tn=512)
    osg_p, v_p = _spatial_gating_prompt(z, S5_WIDTH, bsz, t, ln_g, ln_b, w_s, b_s)
    osg_s, v_s = _spatial_gating_step(z[np_rows:, S5_WIDTH:], ln_g, ln_b, w_s, b_s)
    osg = jnp.concatenate([osg_p, osg_s], 0)
    return o5, osg, (pr, pi, sr, si, v_p, v_s[:, None, :])


def _layer_cd(x, np_rows, bsz, t, pos, layer, caches, page_table, w):
    (w_in, q_norm, kv_norm, w_uq, w_uk, w_uv, lq1, lk1, lq2, lk2, subln) = w
    lat_pool, pet_pool, dkt_pool, dv_pool = caches
    n = x.shape[0]
    ns = n - np_rows
    nh = MLA_HEADS
    page = lat_pool.shape[1]
    a1, a2 = MLA_Q_RANK + MLA_KV_RANK, MLA_Q_RANK + MLA_KV_RANK + MLA_ROPE
    ndq = DIFF_HEADS * 2 * DIFF_HEAD_DIM
    ndk = DIFF_KV_HEADS * 2 * DIFF_HEAD_DIM
    ndv = DIFF_KV_HEADS * DIFF_V_DIM
    w_perm = jnp.concatenate([w_in[:, :a1], w_in[:, a2:], w_in[:, a1:a2],
                              jnp.zeros((w_in.shape[0], 128 - MLA_ROPE), F32)], 1).astype(BF16)
    z = _mm(x, w_perm, tm=640, tn=w_perm.shape[1])
    c = 0
    q_lat = z[:, c:c + MLA_Q_RANK]; c += MLA_Q_RANK
    kv_lat = z[:, c:c + MLA_KV_RANK]; c += MLA_KV_RANK
    dq = z[:, c:c + ndq]; c += ndq
    dk = z[:, c:c + ndk]; c += ndk
    dv = z[:, c:c + ndv]; c += ndv
    k_pe = z[:, c:c + MLA_ROPE]
    q = _mm(_rmsnorm(q_lat, q_norm), w_uq.reshape(MLA_Q_RANK, -1).astype(BF16), tm=640, tn=512)
    q = q.reshape(n, nh, MLA_NOPE + MLA_ROPE)
    q_pe = _rope(q[..., MLA_NOPE:], pos)
    q_abs = _bmm(jnp.transpose(q[..., :MLA_NOPE], (1, 0, 2)),
                 jnp.transpose(w_uk, (1, 2, 0)).astype(BF16), tm=640)
    q_abs = jnp.transpose(q_abs, (1, 0, 2))
    c_kv = _rmsnorm(kv_lat, kv_norm)
    k_pe = _rope(k_pe, pos)
    dq = _rope(dq.reshape(n, DIFF_KV_HEADS, DIFF_GROUP, 2, DIFF_HEAD_DIM), pos)
    dk = _rope(dk.reshape(n, DIFF_KV_HEADS, 2, DIFF_HEAD_DIM), pos)
    dv = dv.reshape(n, DIFF_KV_HEADS, DIFF_V_DIM)
    lam_init = 0.8 - 0.6 * math.exp(-0.3 * layer)
    lam = jnp.exp(jnp.sum(lq1 * lk1)) - jnp.exp(jnp.sum(lq2 * lk2)) + lam_init

    pr = lambda a: a[:np_rows].reshape((bsz, t) + a.shape[1:])
    y_mla_p = _mla_prefill(pr(q_abs), pr(q_pe), pr(c_kv), pr(k_pe), w_uv)
    y_diff_p = _diff_prefill(pr(dq), pr(dk), pr(dv), lam, subln, lam_init)

    sm = lambda a: a[np_rows:]
    mla_scale = (MLA_NOPE + MLA_ROPE) ** -0.5
    s0 = (jnp.einsum('bhr,br->bh', sm(q_abs), sm(c_kv)) + jnp.einsum('bhd,bd->bh', sm(q_pe), sm(k_pe))) * mla_scale
    rpad = DECODE_ROWS - nh
    padr = lambda a: jnp.pad(a, ((0, 0), (0, rpad)) + ((0, 0),) * (a.ndim - 2))
    o_mla = _paged_call(functools.partial(_mla_decode_step, scale=mla_scale), "mla_decode", page_table,
                        [padr(sm(q_abs)).astype(BF16), padr(sm(q_pe)).astype(BF16), padr(s0)[..., None],
                         sm(c_kv)[:, None, :]],
                        [lat_pool, pet_pool], DECODE_ROWS, MLA_KV_RANK)[:, :nh]
    y_mla_s = _bmm(jnp.transpose(o_mla, (1, 0, 2)), jnp.transpose(w_uv, (1, 0, 2)).astype(BF16), tm=ns)
    y_mla_s = jnp.transpose(y_mla_s, (1, 0, 2)).reshape(ns, nh * MLA_V)

    diff_scale = DIFF_HEAD_DIM ** -0.5
    dq_s, dk_s, dv_s = sm(dq), sm(dk), sm(dv)
    qrow = jnp.transpose(dq_s, (0, 1, 3, 2, 4))
    sel = jnp.eye(DIFF_KV_HEADS * 2, dtype=F32).reshape(DIFF_KV_HEADS, 2, 1, DIFF_KV_HEADS * 2, 1)
    qblk = (qrow[:, :, :, :, None, :] * sel[None]).reshape(ns, DECODE_ROWS, DIFF_KV_HEADS * 2 * DIFF_HEAD_DIM)
    s0d = jnp.einsum('brk,bk->br', qblk, dk_s.reshape(ns, -1)) * diff_scale
    o_d = _paged_call(functools.partial(_diff_decode_step, page=page, scale=diff_scale), "diff_decode",
                      page_table, [qblk.astype(BF16), s0d[..., None], dv_s.reshape(ns, 1, -1)],
                      [dkt_pool, dv_pool], DECODE_ROWS, DIFF_KV_HEADS * DIFF_V_DIM)
    o_d = o_d.reshape(ns, DIFF_KV_HEADS, 2, DIFF_GROUP, DIFF_KV_HEADS, DIFF_V_DIM)
    o_d = jnp.stack([o_d[:, h, :, :, h] for h in range(DIFF_KV_HEADS)], 1)
    y_diff_s = _diff_finalize(o_d[:, :, 0], o_d[:, :, 1], lam, subln, lam_init).reshape(ns, -1)

    y_mla = jnp.concatenate([y_mla_p.reshape(np_rows, -1), y_mla_s], 0)
    y_diff = jnp.concatenate([y_diff_p.reshape(np_rows, -1), y_diff_s], 0)
    outs = (pr(c_kv), pr(k_pe), sm(c_kv)[:, None], sm(k_pe)[:, None],
            pr(dk), pr(dv), sm(dk)[:, None], sm(dv)[:, None])
    return y_mla, y_diff, outs


def kernel(x_prompt, x_sample, state_s5_re, state_s5_im, cache_mla_latent, cache_mla_krope, cache_diff_k, cache_diff_v, page_table, p_prompt, p_sample, ab_w_in, s5_lambda_re, s5_lambda_im, s5_b_re, s5_b_im, s5_c_re, s5_c_im, s5_d, s5_log_dt, s5_w_glu, s5_b_glu, sg_ln_g, sg_ln_b, sg_w_s, sg_b_s, ab_w_out, cd_w_in, mla_q_norm, mla_kv_norm, mla_w_uq, mla_w_uk, mla_w_uv, diff_lq1, diff_lk1, diff_lq2, diff_lk2, diff_subln, cd_w_out, ln1_g, ln1_b, ln2_g, ln2_b, router_w, router_b, ex_w_gate, ex_b_gate, ex_w_up, ex_b_up, ex_w_down, ex_b_down, ple_w_proj, ple_w_gate):
    bsz, t, d = x_prompt.shape
    nb, ts, _ = x_sample.shape
    assert ts == 1
    np_rows = bsz * t
    npool, page = cache_mla_latent.shape[1:3]
    past_len = page_table.shape[1] * page
    pos = jnp.concatenate([jnp.tile(jnp.arange(t, dtype=jnp.int32), bsz),
                           jnp.full((nb,), past_len, jnp.int32)])
    x = jnp.concatenate([x_prompt.reshape(np_rows, d), x_sample.reshape(nb, d)], 0)
    p_all = jnp.concatenate([p_prompt.reshape(DEPTH, np_rows, -1), p_sample.reshape(DEPTH, nb, -1)], 1)
    ab_outs, cd_outs = [], []
    for i in range(DEPTH):
        j = i // 2
        if i % 2 == 0:
            w = (ab_w_in[j], s5_lambda_re[j], s5_lambda_im[j], s5_b_re[j], s5_b_im[j], s5_c_re[j], s5_c_im[j],
                 s5_d[j], s5_log_dt[j], s5_w_glu[j], s5_b_glu[j], sg_ln_g[j], sg_ln_b[j], sg_w_s[j], sg_b_s[j])
            a, b, outs = _layer_ab(x, np_rows, bsz, t, state_s5_re[j], state_s5_im[j], w)
            ab_outs.append(outs)
            w_out = ab_w_out[j]
        else:
            w = (cd_w_in[j], mla_q_norm[j], mla_kv_norm[j], mla_w_uq[j], mla_w_uk[j], mla_w_uv[j],
                 diff_lq1[j], diff_lk1[j], diff_lq2[j], diff_lk2[j], diff_subln[j])
            caches = (cache_mla_latent[j].reshape(npool, page, -1),
                      jnp.swapaxes(cache_mla_krope[j], 1, 2),
                      jnp.transpose(cache_diff_k[j], (0, 2, 3, 4, 1)).reshape(npool, -1, page),
                      cache_diff_v[j].reshape(npool, page * DIFF_KV_HEADS, DIFF_V_DIM))
            a, b, outs = _layer_cd(x, np_rows, bsz, t, pos, i, caches, page_table, w)
            cd_outs.append(outs)
            w_out = cd_w_out[j]
        ka = a.shape[1]
        x1, logits = _mix_ln_router(x, a, b, w_out[:ka].astype(BF16), w_out[ka:].astype(BF16),
                                    ln1_g[i], ln1_b[i], router_w[i], router_b[i], tm=520)
        ysel, gates = _moe(x1, logits, i, ex_w_gate, ex_b_gate, ex_w_up, ex_b_up, ex_w_down, ex_b_down)
        x = _post(x1, ysel, gates, p_all[i], ple_w_proj[i].astype(BF16), ple_w_gate[i].astype(BF16),
                  ln2_g[i], ln2_b[i], tm=208)
    y_prompt = x[:np_rows].reshape(bsz, t, d)
    y_sample = x[np_rows:].reshape(nb, ts, d)
    st = lambda k: jnp.stack([o[k] for o in ab_outs])
    sc = lambda k: jnp.stack([o[k] for o in cd_outs])
    return (y_prompt, y_sample, st(0), st(1), st(2), st(3), st(4), st(5),
            sc(0), sc(1), sc(2), sc(3), sc(4), sc(5), sc(6), sc(7))
```

```python
import functools
import math

import jax
import jax.numpy as jnp
from jax import lax
from jax.experimental import pallas as pl
from jax.experimental.pallas import tpu as pltpu

F32 = jnp.float32
BF16 = jnp.bfloat16
I32 = jnp.int32

D_MODEL = 2048
DEPTH = 2
S5_WIDTH = 1024
S5_GROUP = 16
S5_GROUPS = 64
S5_STATE = 64
S5_CHUNK = 16
SG_HEADS = 8
SG_WIDTH = 1024
SG_CHUNK = 128
MLA_HEADS = 8
MLA_Q_RANK = 512
MLA_KV_RANK = 256
MLA_NOPE = 128
MLA_ROPE = 64
MLA_V = 128
DIFF_HEADS = 8
DIFF_KV_HEADS = 2
DIFF_GROUP = 4
DIFF_HEAD_DIM = 64
DIFF_V_DIM = 128
N_EXPERTS = 32
TOP_K = 4
SWIGLU_LIMIT = 7.0
SWIGLU_ALPHA = 1.702
PLE_DIM = 256
ROPE_THETA = 10000.0
Q_BLOCK = 128
ALPHA = (2 * DEPTH) ** 0.25
NEG_INF = -1e30

MOE_SUB = 256
MOE_NSUB = 5
MOE_TF = 256
PAGES_PER_STEP = 32
DECODE_ROWS = 16
VMEM_LIMIT = 56 * 1024 * 1024


MOE_VMEM_LIMIT = 61 * 1024 * 1024


def _cparams(sem, limit=VMEM_LIMIT):
    return pltpu.CompilerParams(dimension_semantics=sem, vmem_limit_bytes=limit)


def _row_tile(rows, target):
    best = 8
    for cand in range(8, min(rows, target) + 1, 8):
        if rows % cand == 0:
            best = cand
    return best


def _gelu(x):
    return 0.5 * x * (1.0 + jnp.tanh(0.7978845608028654 * (x + 0.044715 * (x * x * x))))


def _sigmoid(x):
    return 1.0 / (1.0 + jnp.exp(-x))


def _dot(a, b):
    return jnp.dot(a, b, preferred_element_type=F32)


def _dot_nt(a, b):
    return lax.dot_general(a, b, (((1,), (1,)), ((), ())), preferred_element_type=F32)


def _ln_rows(x, g, b, eps=1e-5):
    mu = jnp.mean(x, -1, keepdims=True)
    xc = x - mu
    var = jnp.mean(xc * xc, -1, keepdims=True)
    return xc * lax.rsqrt(var + eps) * g + b


def _mm_kernel(x_ref, w_ref, o_ref, *, gelu_from):
    acc = _dot(x_ref[...].astype(BF16), w_ref[...])
    if gelu_from is None:
        o_ref[...] = acc
    else:
        j = pl.program_id(0)

        @pl.when(j < gelu_from)
        def _():
            o_ref[...] = acc

        @pl.when(j >= gelu_from)
        def _():
            o_ref[...] = _gelu(acc)


def _mm(x, w, *, tm, tn, gelu_from=None):
    m, k = x.shape
    n = w.shape[1]
    tm = _row_tile(m, tm)
    return pl.pallas_call(
        functools.partial(_mm_kernel, gelu_from=gelu_from),
        grid=(n // tn, m // tm),
        in_specs=[pl.BlockSpec((tm, k), lambda j, i: (i, 0)),
                  pl.BlockSpec((k, tn), lambda j, i: (0, j))],
        out_specs=pl.BlockSpec((tm, tn), lambda j, i: (i, j)),
        out_shape=jax.ShapeDtypeStruct((m, n), F32),
        compiler_params=_cparams(("arbitrary", "arbitrary")),
        name="mm",
    )(x, w)


def _bmm_kernel(x_ref, w_ref, o_ref):
    o_ref[...] = _dot(x_ref[...].astype(BF16), w_ref[...])


def _bmm(x, w, *, tm):
    h, m, k = x.shape
    n = w.shape[2]
    tm = _row_tile(m, tm)
    return pl.pallas_call(
        _bmm_kernel,
        grid=(h, m // tm),
        in_specs=[pl.BlockSpec((None, tm, k), lambda a, i: (a, i, 0)),
                  pl.BlockSpec((None, k, n), lambda a, i: (a, 0, 0))],
        out_specs=pl.BlockSpec((None, tm, n), lambda a, i: (a, i, 0)),
        out_shape=jax.ShapeDtypeStruct((h, m, n), F32),
        compiler_params=_cparams(("arbitrary", "arbitrary")),
        name="bmm",
    )(x, w)


def _glu_kernel(g_ref, gcol_ref, w_ref, b_ref, o_ref):
    acc = _dot(g_ref[...].astype(BF16), w_ref[...]) + b_ref[...]
    o_ref[...] = gcol_ref[...] * _sigmoid(acc)


def _glu(g5, w, b, *, tm, tn):
    m, k = g5.shape
    n = w.shape[1]
    tm = _row_tile(m, tm)
    return pl.pallas_call(
        _glu_kernel,
        grid=(n // tn, m // tm),
        in_specs=[pl.BlockSpec((tm, k), lambda j, i: (i, 0)),
                  pl.BlockSpec((tm, tn), lambda j, i: (i, j)),
                  pl.BlockSpec((k, tn), lambda j, i: (0, j)),
                  pl.BlockSpec((1, tn), lambda j, i: (0, j))],
        out_specs=pl.BlockSpec((tm, tn), lambda j, i: (i, j)),
        out_shape=jax.ShapeDtypeStruct((m, n), F32),
        compiler_params=_cparams(("arbitrary", "arbitrary")),
        name="glu",
    )(g5, g5, w, b)


def _mix_ln_router_kernel(r_ref, a_ref, b_ref, wa_ref, wb_ref, g_ref, beta_ref,
                          wrh_ref, wrl_ref, br_ref, x_ref, lg_ref):
    mix = _dot(a_ref[...].astype(BF16), wa_ref[...]) + _dot(b_ref[...].astype(BF16), wb_ref[...])
    x = _ln_rows(ALPHA * r_ref[...] + mix, g_ref[...], beta_ref[...])
    x_ref[...] = x
    xh = x.astype(BF16)
    xl = (x - xh.astype(F32)).astype(BF16)
    lg_ref[...] = (_dot(xh, wrh_ref[...]) + _dot(xl, wrh_ref[...]) + _dot(xh, wrl_ref[...])
                   + br_ref[...])


def _mix_ln_router(r, a, b, wa, wb, g, beta, wr, br, *, tm):
    m, d = r.shape
    tm = _row_tile(m, tm)
    ka, kb = a.shape[1], b.shape[1]
    ne = wr.shape[1]
    wr_p = jnp.zeros((d, 128), F32).at[:, :ne].set(wr)
    wrh = wr_p.astype(BF16)
    wrl = (wr_p - wrh.astype(F32)).astype(BF16)
    br_p = jnp.zeros((1, 128), F32).at[0, :ne].set(br)
    row = lambda i: (i, 0)
    fix = lambda i: (0, 0)
    x, lg = pl.pallas_call(
        _mix_ln_router_kernel,
        grid=(m // tm,),
        in_specs=[pl.BlockSpec((tm, d), row), pl.BlockSpec((tm, ka), row), pl.BlockSpec((tm, kb), row),
                  pl.BlockSpec((ka, d), fix), pl.BlockSpec((kb, d), fix),
                  pl.BlockSpec((1, d), fix), pl.BlockSpec((1, d), fix),
                  pl.BlockSpec((d, 128), fix), pl.BlockSpec((d, 128), fix), pl.BlockSpec((1, 128), fix)],
        out_specs=[pl.BlockSpec((tm, d), row), pl.BlockSpec((tm, 128), row)],
        out_shape=[jax.ShapeDtypeStruct((m, d), F32), jax.ShapeDtypeStruct((m, 128), F32)],
        compiler_params=_cparams(("arbitrary",)),
        name="mix_ln_router",
    )(r, a, b, wa, wb, g.reshape(1, d), beta.reshape(1, d), wrh, wrl, br_p)
    return x, lg[:, :ne]


def _post_kernel(x1_ref, *refs):
    ys_refs = refs[:TOP_K]
    gt_ref, p_ref, wp_ref, wg_ref, g_ref, beta_ref, o_ref = refs[TOP_K:]
    y = gt_ref[:, 0:1] * ys_refs[0][...]
    for k in range(1, TOP_K):
        y = y + gt_ref[:, k:k + 1] * ys_refs[k][...]
    x = _ln_rows(ALPHA * x1_ref[...] + y, g_ref[...], beta_ref[...])
    proj = _dot(p_ref[...].astype(BF16), wp_ref[...])
    gate = _sigmoid(_dot(x.astype(BF16), wg_ref[...]))
    o_ref[...] = x + proj * gate


def _post(x1, ysel, gates, p, wp, wg, g, beta, *, tm):
    m, d = x1.shape
    tm = _row_tile(m, tm)
    nb = m // tm
    kp = p.shape[1]
    gt = jnp.zeros((m, 128), F32).at[:, :TOP_K].set(gates)
    row = lambda i: (i, 0)
    fix = lambda i: (0, 0)
    ys_specs = [pl.BlockSpec((tm, d), functools.partial(lambda i, k: (k * nb + i, 0), k=k)) for k in range(TOP_K)]
    return pl.pallas_call(
        _post_kernel,
        grid=(nb,),
        in_specs=[pl.BlockSpec((tm, d), row)] + ys_specs + [
            pl.BlockSpec((tm, 128), row), pl.BlockSpec((tm, kp), row), pl.BlockSpec((kp, d), fix),
            pl.BlockSpec((d, d), fix), pl.BlockSpec((1, d), fix), pl.BlockSpec((1, d), fix)],
        out_specs=pl.BlockSpec((tm, d), row),
        out_shape=jax.ShapeDtypeStruct((m, d), F32),
        compiler_params=_cparams(("arbitrary",)),
        name="post",
    )(x1, *([ysel] * TOP_K), gt, p, wp, wg, g.reshape(1, d), beta.reshape(1, d))


def _s5_discretize(lam_re, lam_im, b_re, b_im, log_dt):
    dt = jnp.exp(log_dt)[:, None]
    lr = jnp.minimum(lam_re, -1e-4)
    li = lam_im
    mag = jnp.exp(lr * dt)
    ab_re, ab_im = mag * jnp.cos(li * dt), mag * jnp.sin(li * dt)
    nr = ab_re - 1.0
    den = lr * lr + li * li
    zr = (nr * lr + ab_im * li) / den
    zi = (ab_im * lr - nr * li) / den
    bb_re = zr[..., None] * b_re - zi[..., None] * b_im
    bb_im = zr[..., None] * b_im + zi[..., None] * b_re
    return lr * dt, li * dt, ab_re, ab_im, bb_re, bb_im


def _s5_chunk_operators(lam_re, lam_im, b_re, b_im, c_re, c_im, log_dt):
    L = S5_CHUNK
    G, P, H = S5_GROUPS, S5_STATE, S5_GROUP
    ldr, ldi, _, _, bb_re, bb_im = _s5_discretize(lam_re, lam_im, b_re, b_im, log_dt)
    k = jnp.arange(L + 1, dtype=F32)[:, None, None]
    pk_mag = jnp.exp(k * ldr)
    pk_re, pk_im = pk_mag * jnp.cos(k * ldi), pk_mag * jnp.sin(k * ldi)
    abb_re = pk_re[..., None] * bb_re - pk_im[..., None] * bb_im
    abb_im = pk_re[..., None] * bb_im + pk_im[..., None] * bb_re
    kern = (jnp.einsum('gcp,kgpd->kgcd', c_re, abb_re[:L]) - jnp.einsum('gcp,kgpd->kgcd', c_im, abb_im[:L]))
    t = jnp.arange(L)
    lag = t[None, :] - t[:, None]
    toe = jnp.where((lag >= 0)[:, :, None, None, None], kern[jnp.clip(lag, 0, L - 1)], 0.0)
    m_op = jnp.transpose(toe, (2, 0, 4, 1, 3)).reshape(G, L * H, L * H)
    rev_re = abb_re[L - 1 - t]
    rev_im = abb_im[L - 1 - t]
    bs_op = jnp.concatenate([jnp.transpose(rev_re, (1, 0, 3, 2)).reshape(G, L * H, P),
                             jnp.transpose(rev_im, (1, 0, 3, 2)).reshape(G, L * H, P)], -1)
    ca_re = c_re[None] * pk_re[1:, :, None, :] - c_im[None] * pk_im[1:, :, None, :]
    ca_im = c_re[None] * pk_im[1:, :, None, :] + c_im[None] * pk_re[1:, :, None, :]
    cs_op = jnp.concatenate([jnp.transpose(ca_re, (1, 3, 0, 2)).reshape(G, P, L * H),
                             -jnp.transpose(ca_im, (1, 3, 0, 2)).reshape(G, P, L * H)], 1)
    return m_op, bs_op, cs_op, pk_re[L], pk_im[L]


def _s5_state_kernel(u_ref, bs_ref, s_ref, *, gpb):
    for q in range(gpb):
        s_ref[:, q * 128:(q + 1) * 128] = _dot(u_ref[q].astype(BF16), bs_ref[q])


def _s5_scan_kernel(sr_ref, si_ref, ar_ref, ai_ref, x0r_ref, x0i_ref, er_ref, ei_ref):
    nchunk = sr_ref.shape[0]
    ar, ai = ar_ref[...], ai_ref[...]

    def body(j, carry):
        xr, xi = carry
        x0r_ref[j] = xr
        x0i_ref[j] = xi
        return ar * xr - ai * xi + sr_ref[j], ar * xi + ai * xr + si_ref[j]

    zero = jnp.zeros(sr_ref.shape[1:], F32)
    xr, xi = lax.fori_loop(0, nchunk, body, (zero, zero))
    er_ref[...] = xr
    ei_ref[...] = xi


def _s5_out_kernel(u_ref, x0_ref, m_ref, cs_ref, d_ref, o_ref, *, gpb):
    for q in range(gpb):
        u = u_ref[q]
        y = (_dot(u.astype(BF16), m_ref[q]) + _dot(x0_ref[:, q * 128:(q + 1) * 128].astype(BF16), cs_ref[q])
             + d_ref[q] * u)
        o_ref[q] = _gelu(y)


def _s5_prompt(u, ops, d_skip):
    m_op, bs_op, cs_op, al_re, al_im = ops
    bsz, t, _ = u.shape
    L, G, P, H = S5_CHUNK, S5_GROUPS, S5_STATE, S5_GROUP
    nchunk = t // L
    rows = nchunk * bsz
    gpb = 8
    ug = jnp.transpose(u.reshape(bsz, nchunk, L, G, H), (3, 1, 0, 2, 4)).reshape(G, rows, L * H)
    s_loc = pl.pallas_call(
        functools.partial(_s5_state_kernel, gpb=gpb),
        grid=(G // gpb,),
        in_specs=[pl.BlockSpec((gpb, rows, L * H), lambda g: (g, 0, 0)),
                  pl.BlockSpec((gpb, L * H, 2 * P), lambda g: (g, 0, 0))],
        out_specs=pl.BlockSpec((rows, gpb * 2 * P), lambda g: (0, g)),
        out_shape=jax.ShapeDtypeStruct((rows, G * 2 * P), F32),
        compiler_params=_cparams(("arbitrary",)),
        name="s5_state",
    )(ug, bs_op.astype(BF16))
    s4 = s_loc.reshape(nchunk, bsz, G, 2, P)
    sr = s4[:, :, :, 0].reshape(nchunk, bsz, G * P)
    si = s4[:, :, :, 1].reshape(nchunk, bsz, G * P)
    lw = 512
    seq = pl.BlockSpec((nchunk, bsz, lw), lambda i: (0, 0, i))
    vec = pl.BlockSpec((1, lw), lambda i: (0, i))
    fin = pl.BlockSpec((bsz, lw), lambda i: (0, i))
    x0r, x0i, er, ei = pl.pallas_call(
        _s5_scan_kernel,
        grid=(G * P // lw,),
        in_specs=[seq, seq, vec, vec],
        out_specs=[seq, seq, fin, fin],
        out_shape=[jax.ShapeDtypeStruct((nchunk, bsz, G * P), F32)] * 2
        + [jax.ShapeDtypeStruct((bsz, G * P), F32)] * 2,
        compiler_params=_cparams(("arbitrary",)),
        name="s5_scan",
    )(sr, si, al_re.reshape(1, G * P), al_im.reshape(1, G * P))
    x0 = jnp.stack([x0r.reshape(rows, G, P), x0i.reshape(rows, G, P)], 2).reshape(rows, G * 2 * P)
    d_t = jnp.tile(d_skip, (1, L)).reshape(G, 1, L * H)
    g5 = pl.pallas_call(
        functools.partial(_s5_out_kernel, gpb=gpb),
        grid=(G // gpb,),
        in_specs=[pl.BlockSpec((gpb, rows, L * H), lambda g: (g, 0, 0)),
                  pl.BlockSpec((rows, gpb * 2 * P), lambda g: (0, g)),
                  pl.BlockSpec((gpb, L * H, L * H), lambda g: (g, 0, 0)),
                  pl.BlockSpec((gpb, 2 * P, L * H), lambda g: (g, 0, 0)),
                  pl.BlockSpec((gpb, 1, L * H), lambda g: (g, 0, 0))],
        out_specs=pl.BlockSpec((gpb, rows, L * H), lambda g: (g, 0, 0)),
        out_shape=jax.ShapeDtypeStruct((G, rows, L * H), F32),
        compiler_params=_cparams(("arbitrary",)),
        name="s5_out",
    )(ug, x0, m_op.astype(BF16), cs_op.astype(BF16), d_t)
    g5 = jnp.transpose(g5.reshape(G, nchunk, bsz, L, H), (2, 1, 3, 0, 4)).reshape(bsz, t, G * H)
    return g5, er.reshape(bsz, G, P), ei.reshape(bsz, G, P)


def _s5_step_kernel(u_ref, x0r_ref, x0i_ref, ar_ref, ai_ref, bbr_ref, bbi_ref, cr_ref, ci_ref, d_ref,
                    g_ref, xr_ref, xi_ref, *, nblk, bw, sw):
    for q in range(nblk):
        u = u_ref[:, q * bw:(q + 1) * bw]
        ub = u.astype(BF16)
        sl = slice(q * sw, (q + 1) * sw)
        ar, ai = ar_ref[:, sl], ai_ref[:, sl]
        x0r, x0i = x0r_ref[:, sl], x0i_ref[:, sl]
        xr = ar * x0r - ai * x0i + _dot(ub, bbr_ref[q])
        xi = ar * x0i + ai * x0r + _dot(ub, bbi_ref[q])
        xr_ref[:, sl] = xr
        xi_ref[:, sl] = xi
        y = _dot(xr.astype(BF16), cr_ref[q]) - _dot(xi.astype(BF16), ci_ref[q]) + d_ref[:, q * bw:(q + 1) * bw] * u
        g_ref[:, q * bw:(q + 1) * bw] = _gelu(y)


def _s5_step(u, x0r, x0i, lam_re, lam_im, b_re, b_im, c_re, c_im, log_dt, d_skip):
    n = u.shape[0]
    G, P, H = S5_GROUPS, S5_STATE, S5_GROUP
    gb = 16
    nblk = G // gb
    _, _, ab_re, ab_im, bb_re, bb_im = _s5_discretize(lam_re, lam_im, b_re, b_im, log_dt)
    eye = jnp.eye(gb, dtype=F32)

    def in_blocks(w):
        w4 = w.reshape(nblk, gb, P, H)
        return jnp.einsum('qgph,gk->qghkp', w4, eye).reshape(nblk, gb * H, gb * P)

    def out_blocks(w):
        w4 = w.reshape(nblk, gb, H, P)
        return jnp.einsum('qghp,gk->qgpkh', w4, eye).reshape(nblk, gb * P, gb * H)

    full = lambda i: (0, 0)
    full3 = lambda i: (0, 0, 0)
    g5, xr, xi = pl.pallas_call(
        functools.partial(_s5_step_kernel, nblk=nblk, bw=gb * H, sw=gb * P),
        grid=(1,),
        in_specs=[pl.BlockSpec((n, G * H), full), pl.BlockSpec((n, G * P), full), pl.BlockSpec((n, G * P), full),
                  pl.BlockSpec((1, G * P), full), pl.BlockSpec((1, G * P), full),
                  pl.BlockSpec((nblk, gb * H, gb * P), full3), pl.BlockSpec((nblk, gb * H, gb * P), full3),
                  pl.BlockSpec((nblk, gb * P, gb * H), full3), pl.BlockSpec((nblk, gb * P, gb * H), full3),
                  pl.BlockSpec((1, G * H), full)],
        out_specs=[pl.BlockSpec((n, G * H), full), pl.BlockSpec((n, G * P), full), pl.BlockSpec((n, G * P), full)],
        out_shape=[jax.ShapeDtypeStruct((n, G * H), F32), jax.ShapeDtypeStruct((n, G * P), F32),
                   jax.ShapeDtypeStruct((n, G * P), F32)],
        compiler_params=_cparams(("arbitrary",)),
        name="s5_step",
    )(u, x0r.reshape(n, G * P), x0i.reshape(n, G * P), ab_re.reshape(1, G * P), ab_im.reshape(1, G * P),
      in_blocks(bb_re).astype(BF16), in_blocks(bb_im).astype(BF16),
      out_blocks(c_re).astype(BF16), out_blocks(c_im).astype(BF16), d_skip.reshape(1, G * H))
    return g5, xr.reshape(n, G, P), xi.reshape(n, G, P)


def _sg_kernel(u_ref, v_ref, g_ref, b_ref, w_ref, bias_ref, o_ref, vl_ref, *, cpb):
    i = pl.program_id(0)
    v = _ln_rows(v_ref[...], g_ref[...], b_ref[...])
    vb = v.astype(BF16)
    hd = SG_WIDTH // SG_HEADS
    for h in range(SG_HEADS):
        sl = slice(h * hd, (h + 1) * hd)
        s = _dot(w_ref[h], vb[:, sl]) + bias_ref[:, sl]
        o_ref[:, sl] = u_ref[:, sl] * s

    @pl.when(i % cpb == cpb - 1)
    def _():
        vl_ref[...] = v


def _spatial_gating_prompt(z, col0, bsz, t, ln_g, ln_b, w_s, b_s):
    cpb = t // SG_CHUNK
    cu = col0 // SG_WIDTH
    w = (w_s * jnp.tril(jnp.ones((SG_CHUNK, SG_CHUNK), F32))).astype(BF16)
    bias = jnp.repeat(b_s.T, SG_WIDTH // SG_HEADS, axis=1)
    fix = lambda i: (0, 0)
    return pl.pallas_call(
        functools.partial(_sg_kernel, cpb=cpb),
        grid=(bsz * cpb,),
        in_specs=[pl.BlockSpec((SG_CHUNK, SG_WIDTH), lambda i: (i, cu)),
                  pl.BlockSpec((SG_CHUNK, SG_WIDTH), lambda i: (i, cu + 1)),
                  pl.BlockSpec((1, SG_WIDTH), fix), pl.BlockSpec((1, SG_WIDTH), fix),
                  pl.BlockSpec((SG_HEADS, SG_CHUNK, SG_CHUNK), lambda i: (0, 0, 0)),
                  pl.BlockSpec((SG_CHUNK, SG_WIDTH), fix)],
        out_specs=[pl.BlockSpec((SG_CHUNK, SG_WIDTH), lambda i: (i, 0)),
                   pl.BlockSpec((None, SG_CHUNK, SG_WIDTH), lambda i: (i // cpb, 0, 0))],
        out_shape=[jax.ShapeDtypeStruct((bsz * t, SG_WIDTH), F32),
                   jax.ShapeDtypeStruct((bsz, SG_CHUNK, SG_WIDTH), F32)],
        compiler_params=_cparams(("arbitrary",)),
        name="spatial_gating",
    )(z, z, ln_g.reshape(1, -1), ln_b.reshape(1, -1), w, bias)


def _sg_step_kernel(u_ref, v_ref, g_ref, b_ref, w0_ref, b0_ref, o_ref, vn_ref):
    v = _ln_rows(v_ref[...], g_ref[...], b_ref[...])
    vn_ref[...] = v
    o_ref[...] = u_ref[...] * (w0_ref[...] * v + b0_ref[...])


def _spatial_gating_step(zs, ln_g, ln_b, w_s, b_s):
    n = zs.shape[0]
    hd = SG_WIDTH // SG_HEADS
    w0 = jnp.repeat(w_s[:, 0, 0], hd).reshape(1, SG_WIDTH)
    b0 = jnp.repeat(b_s[:, 0], hd).reshape(1, SG_WIDTH)
    fix = lambda i: (0, 0)
    return pl.pallas_call(
        _sg_step_kernel,
        grid=(1,),
        in_specs=[pl.BlockSpec((n, SG_WIDTH), lambda i: (0, 0)), pl.BlockSpec((n, SG_WIDTH), lambda i: (0, 1)),
                  pl.BlockSpec((1, SG_WIDTH), fix), pl.BlockSpec((1, SG_WIDTH), fix),
                  pl.BlockSpec((1, SG_WIDTH), fix), pl.BlockSpec((1, SG_WIDTH), fix)],
        out_specs=[pl.BlockSpec((n, SG_WIDTH), fix), pl.BlockSpec((n, SG_WIDTH), fix)],
        out_shape=[jax.ShapeDtypeStruct((n, SG_WIDTH), F32)] * 2,
        compiler_params=_cparams(("arbitrary",)),
        name="spatial_gating_step",
    )(zs, zs, ln_g.reshape(1, -1), ln_b.reshape(1, -1), w0, b0)


def _online_softmax_update(s, m_sc, l_sc):
    m_old = m_sc[...]
    m_new = jnp.maximum(m_old, jnp.max(s, -1, keepdims=True))
    a = jnp.exp(m_old - m_new)
    p = jnp.exp(s - m_new)
    l_sc[...] = a * l_sc[...] + jnp.sum(p, -1, keepdims=True)
    m_sc[...] = m_new
    return a, p


def _causal_mask(s, qi, kb, tk):
    row = lax.broadcasted_iota(jnp.int32, s.shape, 0)
    col = lax.broadcasted_iota(jnp.int32, s.shape, 1)
    qpos = qi * Q_BLOCK + (row & (Q_BLOCK - 1))
    return jnp.where(qpos >= kb * tk + col, s, NEG_INF)


def _key_block_cases(qi, kb, tk, step):
    needed = kb * tk <= qi * Q_BLOCK + Q_BLOCK - 1
    crosses = (kb + 1) * tk - 1 > qi * Q_BLOCK

    @pl.when(needed & crosses)
    def _():
        step(True)

    @pl.when(needed & jnp.logical_not(crosses))
    def _():
        step(False)


def _mla_prefill_kernel(ql_ref, qp_ref, ckv_ref, kpe_ref, wuv_ref, y_ref, m_sc, l_sc, acc_sc, *, tk):
    qi, kb = pl.program_id(1), pl.program_id(2)

    @pl.when(kb == 0)
    def _():
        m_sc[...] = jnp.full(m_sc.shape, NEG_INF, F32)
        l_sc[...] = jnp.zeros(l_sc.shape, F32)
        acc_sc[...] = jnp.zeros(acc_sc.shape, F32)

    def step(masked):
        k1 = ckv_ref[...].astype(BF16)
        s = _dot_nt(ql_ref[...], k1) + _dot_nt(qp_ref[...], kpe_ref[...].astype(BF16))
        if masked:
            s = _causal_mask(s, qi, kb, tk)
        a, p = _online_softmax_update(s, m_sc, l_sc)
        acc_sc[...] = a * acc_sc[...] + _dot(p.astype(BF16), k1)

    _key_block_cases(qi, kb, tk, step)

    @pl.when(kb == pl.num_programs(2) - 1)
    def _():
        o = (acc_sc[...] / l_sc[...]).astype(BF16)
        for h in range(MLA_HEADS):
            y_ref[:, h * MLA_V:(h + 1) * MLA_V] = _dot(o[h * Q_BLOCK:(h + 1) * Q_BLOCK],
                                                        wuv_ref[:, h * MLA_V:(h + 1) * MLA_V])


def _mla_prefill(q_lat, q_pe, c_kv, k_pe, w_uv, *, tk=512):
    bsz, t, nh, r = q_lat.shape
    nq = t // Q_BLOCK
    tk = min(tk, t)
    rows = nh * Q_BLOCK

    scale = (MLA_NOPE + MLA_ROPE) ** -0.5

    def blocks(q):
        d = q.shape[-1]
        q = jnp.transpose((q * scale).reshape(bsz, nq, Q_BLOCK, nh, d), (0, 1, 3, 2, 4))
        return q.reshape(bsz, nq, rows, d).astype(BF16)

    kmap = lambda b, qi, kb: (b, jnp.minimum(kb, (qi * Q_BLOCK + Q_BLOCK - 1) // tk), 0)
    return pl.pallas_call(
        functools.partial(_mla_prefill_kernel, tk=tk),
        grid=(bsz, nq, t // tk),
        in_specs=[pl.BlockSpec((None, None, rows, r), lambda b, qi, kb: (b, qi, 0, 0)),
                  pl.BlockSpec((None, None, rows, MLA_ROPE), lambda b, qi, kb: (b, qi, 0, 0)),
                  pl.BlockSpec((None, tk, r), kmap),
                  pl.BlockSpec((None, tk, MLA_ROPE), kmap),
                  pl.BlockSpec((r, nh * MLA_V), lambda b, qi, kb: (0, 0))],
        out_specs=pl.BlockSpec((None, Q_BLOCK, nh * MLA_V), lambda b, qi, kb: (b, qi, 0)),
        out_shape=jax.ShapeDtypeStruct((bsz, t, nh * MLA_V), F32),
        scratch_shapes=[pltpu.VMEM((rows, 1), F32), pltpu.VMEM((rows, 1), F32), pltpu.VMEM((rows, r), F32)],
        compiler_params=_cparams(("arbitrary", "arbitrary", "arbitrary")),
        name="mla_prefill",
    )(blocks(q_lat), blocks(q_pe), c_kv, k_pe, w_uv.reshape(r, nh * MLA_V).astype(BF16))


def _diff_finalize(o0, o1, lam, subln, lam_init):
    o = o0 - lam * o1
    ms = jnp.mean(o * o, -1, keepdims=True)
    return o * lax.rsqrt(ms + 1e-5) * subln * (1.0 - lam_init)


def _diff_prefill_kernel(q_ref, k_ref, v_ref, lam_ref, sub_ref, y_ref, m_sc, l_sc, acc_sc, *, tk, lam_init):
    qi, kb = pl.program_id(1), pl.program_id(2)

    @pl.when(kb == 0)
    def _():
        m_sc[...] = jnp.full(m_sc.shape, NEG_INF, F32)
        l_sc[...] = jnp.zeros(l_sc.shape, F32)
        acc_sc[...] = jnp.zeros(acc_sc.shape, F32)

    def step(masked):
        vb = v_ref[...].astype(BF16)
        for mi in range(2):
            s = _dot_nt(q_ref[mi], k_ref[mi].astype(BF16))
            if masked:
                s = _causal_mask(s, qi, kb, tk)
            a, p = _online_softmax_update(s, m_sc.at[mi], l_sc.at[mi])
            acc_sc[mi] = a * acc_sc[mi] + _dot(p.astype(BF16), vb)

    _key_block_cases(qi, kb, tk, step)

    @pl.when(kb == pl.num_programs(2) - 1)
    def _():
        o = _diff_finalize(acc_sc[0] / l_sc[0], acc_sc[1] / l_sc[1], lam_ref[...], sub_ref[...], lam_init)
        for g in range(DIFF_GROUP):
            y_ref[:, g * DIFF_V_DIM:(g + 1) * DIFF_V_DIM] = o[g * Q_BLOCK:(g + 1) * Q_BLOCK]


def _diff_prefill(dq, dk, dv, lam, subln, lam_init, *, tk=512):
    bsz, t = dq.shape[:2]
    nq = t // Q_BLOCK
    tk = min(tk, t)
    rows = DIFF_GROUP * Q_BLOCK
    d = DIFF_HEAD_DIM
    q = (dq * d ** -0.5).reshape(bsz, nq, Q_BLOCK, DIFF_KV_HEADS, DIFF_GROUP, 2, d)
    q = jnp.transpose(q, (0, 3, 1, 5, 4, 2, 6))
    q = q.reshape(bsz * DIFF_KV_HEADS, nq, 2, rows, d).astype(BF16)
    k = jnp.transpose(dk, (0, 2, 3, 1, 4)).reshape(bsz * DIFF_KV_HEADS, 2, t, d)
    v = jnp.transpose(dv, (0, 2, 1, 3)).reshape(bsz * DIFF_KV_HEADS, t, DIFF_V_DIM)
    last = lambda qi: (qi * Q_BLOCK + Q_BLOCK - 1) // tk
    width = DIFF_GROUP * DIFF_V_DIM
    return pl.pallas_call(
        functools.partial(_diff_prefill_kernel, tk=tk, lam_init=lam_init),
        grid=(bsz * DIFF_KV_HEADS, nq, t // tk),
        in_specs=[pl.BlockSpec((None, None, 2, rows, d), lambda a, qi, kb: (a, qi, 0, 0, 0)),
                  pl.BlockSpec((None, 2, tk, d), lambda a, qi, kb: (a, 0, jnp.minimum(kb, last(qi)), 0)),
                  pl.BlockSpec((None, tk, DIFF_V_DIM), lambda a, qi, kb: (a, jnp.minimum(kb, last(qi)), 0)),
                  pl.BlockSpec((1, DIFF_V_DIM), lambda a, qi, kb: (0, 0)),
                  pl.BlockSpec((1, DIFF_V_DIM), lambda a, qi, kb: (0, 0))],
        out_specs=pl.BlockSpec((None, Q_BLOCK, width),
                               lambda a, qi, kb: (a // DIFF_KV_HEADS, qi, a % DIFF_KV_HEADS)),
        out_shape=jax.ShapeDtypeStruct((bsz, t, DIFF_KV_HEADS * width), F32),
        scratch_shapes=[pltpu.VMEM((2, rows, 1), F32), pltpu.VMEM((2, rows, 1), F32),
                        pltpu.VMEM((2, rows, DIFF_V_DIM), F32)],
        compiler_params=_cparams(("arbitrary", "arbitrary", "arbitrary")),
        name="diff_prefill",
    )(q, k, v, jnp.full((1, DIFF_V_DIM), lam, F32), subln.reshape(1, DIFF_V_DIM))


def _decode_init(s0_ref, v0_ref, m_sc, l_sc, acc_sc):
    m_sc[...] = s0_ref[...]
    l_sc[...] = jnp.ones(l_sc.shape, F32)
    acc_sc[...] = jnp.broadcast_to(v0_ref[...], acc_sc.shape)


def _mla_decode_step(small, pages, m_sc, l_sc, acc_sc, *, npg, scale):
    q1_ref, q2_ref = small[0], small[1]
    lat_ref, pet_ref = pages
    q1, q2 = q1_ref[...], q2_ref[...]
    lats, scores = [], []
    for i in range(npg):
        lat = lat_ref[i].astype(BF16)
        scores.append(_dot_nt(q1, lat) + _dot(q2, pet_ref[i].astype(BF16)))
        lats.append(lat)
    s = jnp.concatenate(scores, axis=1) * scale
    a, p = _online_softmax_update(s, m_sc, l_sc)
    pb = p.astype(BF16)
    pw = scores[0].shape[1]
    acc = a * acc_sc[...]
    for i in range(npg):
        acc = acc + _dot(pb[:, i * pw:(i + 1) * pw], lats[i])
    acc_sc[...] = acc


def _diff_decode_step(small, pages, m_sc, l_sc, acc_sc, *, npg, page, scale):
    q = small[0][...]
    kt_ref, v_ref = pages
    scores = [_dot(q, kt_ref[i].astype(BF16)) for i in range(npg)]
    s = jnp.concatenate(scores, axis=1) * scale
    a, p = _online_softmax_update(s, m_sc, l_sc)
    pb = p.astype(BF16)
    for h in range(DIFF_KV_HEADS):
        sl = slice(h * DIFF_V_DIM, (h + 1) * DIFF_V_DIM)
        acc = a * acc_sc[:, sl]
        for i in range(npg):
            vh = v_ref[i, pl.ds(h, page, stride=DIFF_KV_HEADS), :].astype(BF16)
            acc = acc + _dot(pb[:, i * page:(i + 1) * page], vh)
        acc_sc[:, sl] = acc


def _paged_kernel(pt_ref, *refs, n_small, n_pools, npg, step_fn):
    small = refs[:n_small]
    pools = refs[n_small:n_small + n_pools]
    o_ref = refs[n_small + n_pools]
    bufs = refs[n_small + n_pools + 1:n_small + 2 * n_pools + 1]
    sem, m_sc, l_sc, acc_sc = refs[n_small + 2 * n_pools + 1:]
    b, sg = pl.program_id(0), pl.program_id(1)
    nb, ng = pl.num_programs(0), pl.num_programs(1)

    def page_copy(k, i, slot, pid):
        return pltpu.make_async_copy(pools[k].at[pid], bufs[k].at[slot, i], sem.at[k, slot])

    def start_group(bb, grp, slot):
        for k in range(n_pools):
            for i in range(npg):
                page_copy(k, i, slot, pt_ref[bb, grp * npg + i]).start()

    def wait_group(slot):
        for k in range(n_pools):
            for i in range(npg):
                page_copy(k, i, slot, 0).wait()

    t = b * ng + sg
    pair = t % 2

    def start_step(tt, pr):
        bb, gg = tt // ng, tt % ng
        start_group(bb, 2 * gg, 2 * pr)
        start_group(bb, 2 * gg + 1, 2 * pr + 1)

    @pl.when(t == 0)
    def _():
        start_step(0, 0)

        @pl.when(nb * ng > 1)
        def _():
            start_step(1, 1)

    @pl.when(sg == 0)
    def _():
        _decode_init(small[-2], small[-1], m_sc, l_sc, acc_sc)

    for half in range(2):
        wait_group(2 * pair + half)
        step_fn(small, [buf.at[2 * pair + half] for buf in bufs], m_sc, l_sc, acc_sc)

    @pl.when(sg == ng - 1)
    def _():
        o_ref[...] = acc_sc[...] / l_sc[...]

    @pl.when(t + 2 < nb * ng)
    def _():
        start_step(t + 2, pair)


def _paged_call(step_fn, name, page_table, small, pools, r, dv):
    bsz, npages = page_table.shape
    assert npages % 2 == 0
    npg = math.gcd(PAGES_PER_STEP, npages // 2)
    per_b = lambda b, s, pt: (b, 0, 0)
    in_specs = [pl.BlockSpec((None,) + a.shape[1:], per_b) for a in small]
    in_specs += [pl.BlockSpec(memory_space=pl.ANY) for _ in pools]
    grid_spec = pltpu.PrefetchScalarGridSpec(
        num_scalar_prefetch=1,
        grid=(bsz, npages // (2 * npg)),
        in_specs=in_specs,
        out_specs=pl.BlockSpec((None, r, dv), per_b),
        scratch_shapes=[pltpu.VMEM((4, npg) + pool.shape[1:], pool.dtype) for pool in pools]
        + [pltpu.SemaphoreType.DMA((len(pools), 4)),
           pltpu.VMEM((r, 1), F32), pltpu.VMEM((r, 1), F32), pltpu.VMEM((r, dv), F32)],
    )
    return pl.pallas_call(
        functools.partial(_paged_kernel, n_small=len(small), n_pools=len(pools), npg=npg,
                          step_fn=functools.partial(step_fn, npg=npg)),
        grid_spec=grid_spec,
        out_shape=jax.ShapeDtypeStruct((bsz, r, dv), F32),
        compiler_params=_cparams(("arbitrary", "arbitrary")),
        name=name,
    )(page_table, *small, *pools)


def _moe_kernel(first_ref, e_ref, nsub_ref, oblk_ref, *refs):
    del first_ref, e_ref, oblk_ref
    x_refs = refs[:MOE_NSUB]
    wg_ref, bg_ref, wu_ref, bu_ref, wd_ref, bd_ref, o_ref, wg_sc, wu_sc, wd_sc = refs[MOE_NSUB:]
    s, j = pl.program_id(0), pl.program_id(1)
    ns = nsub_ref[s]

    @pl.when((ns == 0) & (j == 0))
    def _():
        o_ref[...] = jnp.zeros(o_ref.shape, F32)

    @pl.when(ns > 0)
    def _():
        @pl.when(j == 0)
        def _():
            o_ref[...] = jnp.broadcast_to(bd_ref[...], o_ref.shape)

        for cnt in range(1, MOE_NSUB + 1):
            @pl.when(ns == cnt)
            def _(cnt=cnt):
                wg_sc[...] = wg_ref[...].astype(BF16)
                wu_sc[...] = wu_ref[...].astype(BF16)
                wd_sc[...] = wd_ref[...].astype(BF16)
                for i in range(cnt):
                    xb = x_refs[i][...].astype(BF16)
                    g = jnp.minimum(_dot(xb, wg_sc[...]) + bg_ref[...], SWIGLU_LIMIT)
                    u = jnp.clip(_dot(xb, wu_sc[...]) + bu_ref[...], -SWIGLU_LIMIT, SWIGLU_LIMIT)
                    h = (u + 1.0) * (g * _sigmoid(SWIGLU_ALPHA * g))
                    o_ref[i * MOE_SUB:(i + 1) * MOE_SUB, :] += _dot(h.astype(BF16), wd_sc[...])


def _moe(x, logits, layer, wg, bg, wu, bu, wd, bd):
    n, d = x.shape
    f = wg.shape[-1]
    sub, nsub_max, tf = MOE_SUB, MOE_NSUB, MOE_TF
    tm = sub * nsub_max
    nf = f // tf
    nk = n * TOP_K
    nblk = nk // sub + N_EXPERTS
    nsb = (nblk + (nsub_max - 1) * N_EXPERTS) // nsub_max
    top_v, top_i = lax.top_k(logits, TOP_K)
    gates = jax.nn.softmax(top_v, axis=-1)
    flat_e = top_i.reshape(-1).astype(I32)
    order = jnp.argsort(flat_e).astype(I32)
    inv = jnp.argsort(order).astype(I32)
    eids = jnp.arange(N_EXPERTS, dtype=I32)
    counts = jnp.sum((flat_e[:, None] == eids[None, :]).astype(I32), axis=0)
    grp_start = (jnp.cumsum(counts) - counts).astype(I32)
    blocks_e = (counts + sub - 1) // sub
    blk_end = jnp.cumsum(blocks_e).astype(I32)
    blk_start = blk_end - blocks_e
    sb_per_e = (blocks_e + nsub_max - 1) // nsub_max
    sb_end = jnp.cumsum(sb_per_e).astype(I32)
    sb_start = sb_end - sb_per_e
    r = jnp.arange(nblk * sub, dtype=I32)
    e_r = jnp.minimum(jnp.sum((r[:, None] // sub >= blk_end[None, :]).astype(I32), axis=1), N_EXPERTS - 1)
    rank_r = r - blk_start[e_r] * sub
    valid_r = (r // sub < blk_end[-1]) & (rank_r < counts[e_r])
    tok = jnp.where(valid_r, order[jnp.clip(grp_start[e_r] + rank_r, 0, nk - 1)] // TOP_K, 0)
    xs = x[tok]
    n_used = sb_end[-1]
    sb_ids = jnp.arange(nsb, dtype=I32)
    sbc = jnp.minimum(sb_ids, n_used - 1)
    sb_e = jnp.minimum(jnp.sum((sbc[:, None] >= sb_end[None, :]).astype(I32), axis=1), N_EXPERTS - 1)
    kk = sbc - sb_start[sb_e]
    first = blk_start[sb_e] + kk * nsub_max
    nsub = jnp.where(sb_ids < n_used, jnp.clip(blocks_e[sb_e] - kk * nsub_max, 0, nsub_max), 0).astype(I32)

    jj = lambda s, j, ns: jnp.where(ns[s] > 0, j, nf - 1)

    def x_spec(i):
        return pl.BlockSpec((sub, d), lambda s, j, fr, e, ns, ob: (jnp.minimum(fr[s] + i, nblk - 1), 0))

    grid_spec = pltpu.PrefetchScalarGridSpec(
        num_scalar_prefetch=4,
        grid=(nsb, nf),
        in_specs=[x_spec(i) for i in range(nsub_max)] + [
            pl.BlockSpec((None, None, d, tf), lambda s, j, fr, e, ns, ob: (layer, e[s], 0, jj(s, j, ns))),
            pl.BlockSpec((None, None, 1, tf), lambda s, j, fr, e, ns, ob: (layer, e[s], 0, jj(s, j, ns))),
            pl.BlockSpec((None, None, d, tf), lambda s, j, fr, e, ns, ob: (layer, e[s], 0, jj(s, j, ns))),
            pl.BlockSpec((None, None, 1, tf), lambda s, j, fr, e, ns, ob: (layer, e[s], 0, jj(s, j, ns))),
            pl.BlockSpec((None, None, tf, d), lambda s, j, fr, e, ns, ob: (layer, e[s], jj(s, j, ns), 0)),
            pl.BlockSpec((None, None, 1, d), lambda s, j, fr, e, ns, ob: (layer, e[s], 0, 0))],
        out_specs=pl.BlockSpec((tm, d), lambda s, j, fr, e, ns, ob: (ob[s], 0)),
        scratch_shapes=[pltpu.VMEM((d, tf), BF16), pltpu.VMEM((d, tf), BF16), pltpu.VMEM((tf, d), BF16)],
    )
    nl, ne = wg.shape[:2]
    yb = pl.pallas_call(
        _moe_kernel,
        grid_spec=grid_spec,
        out_shape=jax.ShapeDtypeStruct((nsb * tm, d), F32),
        compiler_params=_cparams(("arbitrary", "arbitrary"), MOE_VMEM_LIMIT),
        name="moe_experts",
    )(first.astype(I32), sb_e.astype(I32), nsub, sb_ids, *([xs] * nsub_max),
      wg, bg.reshape(nl, ne, 1, f), wu, bu.reshape(nl, ne, 1, f), wd, bd.reshape(nl, ne, 1, d))
    rank_a = inv - grp_start[flat_e]
    dest = (sb_start[flat_e] + rank_a // tm) * tm + rank_a % tm
    return yb[dest.reshape(n, TOP_K).T.reshape(-1)], gates


def _rmsnorm(x, g, eps=1e-6):
    return x * lax.rsqrt(jnp.mean(x * x, -1, keepdims=True) + eps) * g


def _rope(x, pos):
    d = x.shape[-1]
    inv = ROPE_THETA ** (-jnp.arange(0, d, 2, dtype=F32) / d)
    ang = pos.astype(F32)[:, None] * inv[None, :]
    ang = ang.reshape((ang.shape[0],) + (1,) * (x.ndim - 2) + (ang.shape[1],))
    cos, sin = jnp.cos(ang), jnp.sin(ang)
    x1, x2 = jnp.split(x, 2, axis=-1)
    return jnp.concatenate([x1 * cos - x2 * sin, x1 * sin + x2 * cos], -1)


def _layer_ab(x, np_rows, bsz, t, state_re, state_im, w):
    (w_in, lam_re, lam_im, b_re, b_im, c_re, c_im, d_skip, log_dt, w_glu, b_glu,
     ln_g, ln_b, w_s, b_s) = w
    z = _mm(x, w_in.astype(BF16), tm=640, tn=S5_WIDTH, gelu_from=1)
    ops = _s5_chunk_operators(lam_re, lam_im, b_re, b_im, c_re, c_im, log_dt)
    g5_p, pr, pi = _s5_prompt(z[:np_rows, :S5_WIDTH].reshape(bsz, t, S5_WIDTH), ops, d_skip)
    g5_s, sr, si = _s5_step(z[np_rows:, :S5_WIDTH], state_re, state_im,
                            lam_re, lam_im, b_re, b_im, c_re, c_im, log_dt, d_skip)
    g5 = jnp.concatenate([g5_p.reshape(np_rows, S5_WIDTH), g5_s], 0)
    o5 = _glu(g5, w_glu.astype(BF16), b_glu.reshape(1, -1), tm=640, tn=512)
    osg_p, v_p = _spatial_gating_prompt(z, S5_WIDTH, bsz, t, ln_g, ln_b, w_s, b_s)
    osg_s, v_s = _spatial_gating_step(z[np_rows:, S5_WIDTH:], ln_g, ln_b, w_s, b_s)
    osg = jnp.concatenate([osg_p, osg_s], 0)
    return o5, osg, (pr, pi, sr, si, v_p, v_s[:, None, :])


def _layer_cd(x, np_rows, bsz, t, pos, layer, caches, page_table, w):
    (w_in, q_norm, kv_norm, w_uq, w_uk, w_uv, lq1, lk1, lq2, lk2, subln) = w
    lat_pool, pet_pool, dkt_pool, dv_pool = caches
    n = x.shape[0]
    ns = n - np_rows
    nh = MLA_HEADS
    page = lat_pool.shape[1]
    a1, a2 = MLA_Q_RANK + MLA_KV_RANK, MLA_Q_RANK + MLA_KV_RANK + MLA_ROPE
    ndq = DIFF_HEADS * 2 * DIFF_HEAD_DIM
    ndk = DIFF_KV_HEADS * 2 * DIFF_HEAD_DIM
    ndv = DIFF_KV_HEADS * DIFF_V_DIM
    w_perm = jnp.concatenate([w_in[:, :a1], w_in[:, a2:], w_in[:, a1:a2],
                              jnp.zeros((w_in.shape[0], 128 - MLA_ROPE), F32)], 1).astype(BF16)
    z = _mm(x, w_perm, tm=640, tn=w_perm.shape[1])
    c = 0
    q_lat = z[:, c:c + MLA_Q_RANK]; c += MLA_Q_RANK
    kv_lat = z[:, c:c + MLA_KV_RANK]; c += MLA_KV_RANK
    dq = z[:, c:c + ndq]; c += ndq
    dk = z[:, c:c + ndk]; c += ndk
    dv = z[:, c:c + ndv]; c += ndv
    k_pe = z[:, c:c + MLA_ROPE]
    q = _mm(_rmsnorm(q_lat, q_norm), w_uq.reshape(MLA_Q_RANK, -1).astype(BF16), tm=640, tn=512)
    q = q.reshape(n, nh, MLA_NOPE + MLA_ROPE)
    q_pe = _rope(q[..., MLA_NOPE:], pos)
    q_abs = _bmm(jnp.transpose(q[..., :MLA_NOPE], (1, 0, 2)),
                 jnp.transpose(w_uk, (1, 2, 0)).astype(BF16), tm=640)
    q_abs = jnp.transpose(q_abs, (1, 0, 2))
    c_kv = _rmsnorm(kv_lat, kv_norm)
    k_pe = _rope(k_pe, pos)
    dq = _rope(dq.reshape(n, DIFF_KV_HEADS, DIFF_GROUP, 2, DIFF_HEAD_DIM), pos)
    dk = _rope(dk.reshape(n, DIFF_KV_HEADS, 2, DIFF_HEAD_DIM), pos)
    dv = dv.reshape(n, DIFF_KV_HEADS, DIFF_V_DIM)
    lam_init = 0.8 - 0.6 * math.exp(-0.3 * layer)
    lam = jnp.exp(jnp.sum(lq1 * lk1)) - jnp.exp(jnp.sum(lq2 * lk2)) + lam_init

    pr = lambda a: a[:np_rows].reshape((bsz, t) + a.shape[1:])
    y_mla_p = _mla_prefill(pr(q_abs), pr(q_pe), pr(c_kv), pr(k_pe), w_uv)
    y_diff_p = _diff_prefill(pr(dq), pr(dk), pr(dv), lam, subln, lam_init)

    sm = lambda a: a[np_rows:]
    mla_scale = (MLA_NOPE + MLA_ROPE) ** -0.5
    s0 = (jnp.einsum('bhr,br->bh', sm(q_abs), sm(c_kv)) + jnp.einsum('bhd,bd->bh', sm(q_pe), sm(k_pe))) * mla_scale
    rpad = DECODE_ROWS - nh
    padr = lambda a: jnp.pad(a, ((0, 0), (0, rpad)) + ((0, 0),) * (a.ndim - 2))
    o_mla = _paged_call(functools.partial(_mla_decode_step, scale=mla_scale), "mla_decode", page_table,
                        [padr(sm(q_abs)).astype(BF16), padr(sm(q_pe)).astype(BF16), padr(s0)[..., None],
                         sm(c_kv)[:, None, :]],
                        [lat_pool, pet_pool], DECODE_ROWS, MLA_KV_RANK)[:, :nh]
    y_mla_s = _bmm(jnp.transpose(o_mla, (1, 0, 2)), jnp.transpose(w_uv, (1, 0, 2)).astype(BF16), tm=ns)
    y_mla_s = jnp.transpose(y_mla_s, (1, 0, 2)).reshape(ns, nh * MLA_V)

    diff_scale = DIFF_HEAD_DIM ** -0.5
    dq_s, dk_s, dv_s = sm(dq), sm(dk), sm(dv)
    qrow = jnp.transpose(dq_s, (0, 1, 3, 2, 4))
    sel = jnp.eye(DIFF_KV_HEADS * 2, dtype=F32).reshape(DIFF_KV_HEADS, 2, 1, DIFF_KV_HEADS * 2, 1)
    qblk = (qrow[:, :, :, :, None, :] * sel[None]).reshape(ns, DECODE_ROWS, DIFF_KV_HEADS * 2 * DIFF_HEAD_DIM)
    s0d = jnp.einsum('brk,bk->br', qblk, dk_s.reshape(ns, -1)) * diff_scale
    o_d = _paged_call(functools.partial(_diff_decode_step, page=page, scale=diff_scale), "diff_decode",
                      page_table, [qblk.astype(BF16), s0d[..., None], dv_s.reshape(ns, 1, -1)],
                      [dkt_pool, dv_pool], DECODE_ROWS, DIFF_KV_HEADS * DIFF_V_DIM)
    o_d = o_d.reshape(ns, DIFF_KV_HEADS, 2, DIFF_GROUP, DIFF_KV_HEADS, DIFF_V_DIM)
    o_d = jnp.stack([o_d[:, h, :, :, h] for h in range(DIFF_KV_HEADS)], 1)
    y_diff_s = _diff_finalize(o_d[:, :, 0], o_d[:, :, 1], lam, subln, lam_init).reshape(ns, -1)

    y_mla = jnp.concatenate([y_mla_p.reshape(np_rows, -1), y_mla_s], 0)
    y_diff = jnp.concatenate([y_diff_p.reshape(np_rows, -1), y_diff_s], 0)
    outs = (pr(c_kv), pr(k_pe), sm(c_kv)[:, None], sm(k_pe)[:, None],
            pr(dk), pr(dv), sm(dk)[:, None], sm(dv)[:, None])
    return y_mla, y_diff, outs


def kernel(x_prompt, x_sample, state_s5_re, state_s5_im, cache_mla_latent, cache_mla_krope, cache_diff_k, cache_diff_v, page_table, p_prompt, p_sample, ab_w_in, s5_lambda_re, s5_lambda_im, s5_b_re, s5_b_im, s5_c_re, s5_c_im, s5_d, s5_log_dt, s5_w_glu, s5_b_glu, sg_ln_g, sg_ln_b, sg_w_s, sg_b_s, ab_w_out, cd_w_in, mla_q_norm, mla_kv_norm, mla_w_uq, mla_w_uk, mla_w_uv, diff_lq1, diff_lk1, diff_lq2, diff_lk2, diff_subln, cd_w_out, ln1_g, ln1_b, ln2_g, ln2_b, router_w, router_b, ex_w_gate, ex_b_gate, ex_w_up, ex_b_up, ex_w_down, ex_b_down, ple_w_proj, ple_w_gate):
    bsz, t, d = x_prompt.shape
    nb, ts, _ = x_sample.shape
    assert ts == 1
    np_rows = bsz * t
    npool, page = cache_mla_latent.shape[1:3]
    past_len = page_table.shape[1] * page
    pos = jnp.concatenate([jnp.tile(jnp.arange(t, dtype=jnp.int32), bsz),
                           jnp.full((nb,), past_len, jnp.int32)])
    x = jnp.concatenate([x_prompt.reshape(np_rows, d), x_sample.reshape(nb, d)], 0)
    p_all = jnp.concatenate([p_prompt.reshape(DEPTH, np_rows, -1), p_sample.reshape(DEPTH, nb, -1)], 1)
    ab_outs, cd_outs = [], []
    for i in range(DEPTH):
        j = i // 2
        if i % 2 == 0:
            w = (ab_w_in[j], s5_lambda_re[j], s5_lambda_im[j], s5_b_re[j], s5_b_im[j], s5_c_re[j], s5_c_im[j],
                 s5_d[j], s5_log_dt[j], s5_w_glu[j], s5_b_glu[j], sg_ln_g[j], sg_ln_b[j], sg_w_s[j], sg_b_s[j])
            a, b, outs = _layer_ab(x, np_rows, bsz, t, state_s5_re[j], state_s5_im[j], w)
            ab_outs.append(outs)
            w_out = ab_w_out[j]
        else:
            w = (cd_w_in[j], mla_q_norm[j], mla_kv_norm[j], mla_w_uq[j], mla_w_uk[j], mla_w_uv[j],
                 diff_lq1[j], diff_lk1[j], diff_lq2[j], diff_lk2[j], diff_subln[j])
            caches = (cache_mla_latent[j].reshape(npool, page, -1),
                      jnp.swapaxes(cache_mla_krope[j], 1, 2),
                      jnp.transpose(cache_diff_k[j], (0, 2, 3, 4, 1)).reshape(npool, -1, page),
                      cache_diff_v[j].reshape(npool, page * DIFF_KV_HEADS, DIFF_V_DIM))
            a, b, outs = _layer_cd(x, np_rows, bsz, t, pos, i, caches, page_table, w)
            cd_outs.append(outs)
            w_out = cd_w_out[j]
        ka = a.shape[1]
        x1, logits = _mix_ln_router(x, a, b, w_out[:ka].astype(BF16), w_out[ka:].astype(BF16),
                                    ln1_g[i], ln1_b[i], router_w[i], router_b[i], tm=520)
        ysel, gates = _moe(x1, logits, i, ex_w_gate, ex_b_gate, ex_w_up, ex_b_up, ex_w_down, ex_b_down)
        x = _post(x1, ysel, gates, p_all[i], ple_w_proj[i].astype(BF16), ple_w_gate[i].astype(BF16),
                  ln2_g[i], ln2_b[i], tm=208)
    y_prompt = x[:np_rows].reshape(bsz, t, d)
    y_sample = x[np_rows:].reshape(nb, ts, d)
    st = lambda k: jnp.stack([o[k] for o in ab_outs])
    sc = lambda k: jnp.stack([o[k] for o in cd_outs])
    return (y_prompt, y_sample, st(0), st(1), st(2), st(3), st(4), st(5),
            sc(0), sc(1), sc(2), sc(3), sc(4), sc(5), sc(6), sc(7))
```
